```python
import jax, jax.numpy as jnp
from jax import lax
import numpy as np


D_MODEL = 1024
BATCH = 8
SEQ = 2048
DEPTH = 1
DEC_BATCH = 128
DEC_SEQ = 8
PAST_LEN = 16384
PAGE_SIZE = 128

GDN_HEADS = 8
GDN_DK = 128
GDN_DV = 128
GDN_QK_W = GDN_HEADS * GDN_DK
GDN_V_W = GDN_HEADS * GDN_DV
GDN_CONV = 4
GDN_CHUNK = 64
SC_W = D_MODEL
SC_CONV = 3
X_HEADS = 4
X_HEAD_DIM = D_MODEL // X_HEADS
N_MEM = 256
D_FF = 4 * D_MODEL
EPS = 1e-6
IN_SIZES = (GDN_QK_W, GDN_QK_W, GDN_V_W, GDN_V_W, GDN_HEADS, GDN_HEADS, SC_W, SC_W, SC_W, D_MODEL, D_MODEL)
IN_COLS = sum(IN_SIZES)
GDN_CONV_CH = 2 * GDN_QK_W + GDN_V_W

kernel_name = 'hybrid_gdn_shortconv_xattn_step'


def rmsnorm(x, w):
    xf = x.astype(jnp.float32)
    xf = xf * lax.rsqrt(jnp.mean(xf * xf, axis=-1, keepdims=True) + EPS)
    return (xf * w.astype(jnp.float32)).astype(x.dtype)


def l2norm(x):
    xf = x.astype(jnp.float32)
    return xf * lax.rsqrt(jnp.sum(xf * xf, axis=-1, keepdims=True) + EPS)


def causal_dwconv(x, buf, w):
    width = w.shape[0]
    T = x.shape[1]
    xx = jnp.concatenate([buf.astype(x.dtype), x], axis=1)
    y = xx[:, 0:T] * w[0]
    for i in range(1, width):
        y = y + xx[:, i:i + T] * w[i]
    return y, xx[:, xx.shape[1] - (width - 1):]


def split_cols(proj):
    idx = []
    acc = 0
    for s in IN_SIZES[:-1]:
        acc += s
        idx.append(acc)
    return jnp.split(proj, idx, axis=-1)


def gated_delta_rule(q, k, v, g, beta, S0):
    B, T, H, DK = q.shape
    DV = v.shape[-1]
    C = min(GDN_CHUNK, T)
    n = -(-T // C)
    pad = n * C - T
    q = jnp.pad(q, ((0, 0), (0, pad), (0, 0), (0, 0)))
    k = jnp.pad(k, ((0, 0), (0, pad), (0, 0), (0, 0)))
    v = jnp.pad(v, ((0, 0), (0, pad), (0, 0), (0, 0)))
    g = jnp.pad(g, ((0, 0), (0, pad), (0, 0)))
    beta = jnp.pad(beta, ((0, 0), (0, pad), (0, 0)))
    to_chunks4 = lambda t: t.reshape(B, n, C, H, t.shape[-1]).transpose(1, 0, 3, 2, 4)
    to_chunks3 = lambda t: t.reshape(B, n, C, H).transpose(1, 0, 3, 2)
    q, k, v = to_chunks4(q), to_chunks4(k), to_chunks4(v)
    g, beta = to_chunks3(g), to_chunks3(beta)
    gc = jnp.cumsum(g, axis=-1)
    incl = jnp.tril(jnp.ones((C, C), dtype=bool))
    strict = jnp.tril(jnp.ones((C, C), dtype=bool), -1)
    decay = jnp.exp(jnp.where(incl, gc[..., :, None] - gc[..., None, :], -jnp.inf))
    kk = jnp.einsum('nbhck,nbhdk->nbhcd', k, k)
    L = jnp.where(strict, beta[..., :, None] * kk * decay, 0.0)
    A = L + jnp.eye(C, dtype=L.dtype)
    kb_e = k * (beta * jnp.exp(gc))[..., None]
    vb = v * beta[..., None]
    W = lax.linalg.triangular_solve(A, kb_e, left_side=True, lower=True, unit_diagonal=True)
    Uv = lax.linalg.triangular_solve(A, vb, left_side=True, lower=True, unit_diagonal=True)
    P = jnp.einsum('nbhck,nbhdk->nbhcd', q, k) * decay
    q_e = q * jnp.exp(gc)[..., None]
    k_e = k * jnp.exp(gc[..., -1:] - gc)[..., None]
    g_tot = jnp.exp(gc[..., -1])

    def step(S, xs):
        W_c, Uv_c, P_c, qe_c, ke_c, gt_c = xs
        U = Uv_c - jnp.einsum('bhck,bhkv->bhcv', W_c, S)
        O = jnp.einsum('bhck,bhkv->bhcv', qe_c, S) + jnp.einsum('bhcd,bhdv->bhcv', P_c, U)
        S_new = gt_c[..., None, None] * S + jnp.einsum('bhck,bhcv->bhkv', ke_c, U)
        return S_new, O

    S_fin, O = lax.scan(step, S0, (W, Uv, P, q_e, k_e, g_tot))
    O = O.transpose(1, 0, 3, 2, 4).reshape(B, n * C, H, DV)[:, :T]
    return O, S_fin


def memory_kv(mem, mem_norm_w, w_xkv):
    B = mem.shape[0]
    kv = rmsnorm(mem, mem_norm_w) @ w_xkv
    mk, mv = jnp.split(kv, 2, axis=-1)
    return mk.reshape(B, N_MEM, X_HEADS, X_HEAD_DIM), mv.reshape(B, N_MEM, X_HEADS, X_HEAD_DIM)


def hybrid_layer(x, mem_k, mem_v, gdn_conv_buf, gdn_S, sc_buf,
                 norm_mix_w, w_in, gdn_conv_w, gdn_A_log, gdn_dt_bias, gdn_out_norm_w, w_gdn_o,
                 sc_conv_w, w_sc_o, w_mix_out, norm_x_w, w_xq, w_xo, norm_mlp_w, w_mlp_up, w_mlp_down):
    B, T, _ = x.shape
    f32 = jnp.float32
    h = rmsnorm(x, norm_mix_w)
    q, k, v, z, a, b, bg, cg, xin, ga, gb = split_cols(h @ w_in)
    qkv, new_gdn_conv_buf = causal_dwconv(jnp.concatenate([q, k, v], axis=-1), gdn_conv_buf, gdn_conv_w)
    qkv = jax.nn.silu(qkv)
    q, k, v = jnp.split(qkv, [GDN_QK_W, 2 * GDN_QK_W], axis=-1)
    q = l2norm(q.reshape(B, T, GDN_HEADS, GDN_DK)) * (GDN_DK ** -0.5)
    k = l2norm(k.reshape(B, T, GDN_HEADS, GDN_DK))
    v = v.reshape(B, T, GDN_HEADS, GDN_DV).astype(f32)
    g = -jnp.exp(gdn_A_log.astype(f32)) * jax.nn.softplus(a.astype(f32) + gdn_dt_bias.astype(f32))
    beta = jax.nn.sigmoid(b.astype(f32))
    o, new_S = gated_delta_rule(q, k, v, g, beta, gdn_S.astype(f32))
    o = rmsnorm(o, gdn_out_norm_w) * jax.nn.silu(z.reshape(B, T, GDN_HEADS, GDN_DV).astype(f32))
    y_a = o.reshape(B, T, GDN_V_W).astype(x.dtype) @ w_gdn_o
    u, new_sc_buf = causal_dwconv(cg * xin, sc_buf, sc_conv_w)
    y_b = (bg * u) @ w_sc_o
    merged = jax.nn.sigmoid(ga) * y_a + jax.nn.sigmoid(gb) * y_b
    x = x + merged @ w_mix_out
    hq = (rmsnorm(x, norm_x_w) @ w_xq).reshape(B, T, X_HEADS, X_HEAD_DIM)
    s = jnp.einsum('bthd,bshd->bhts', hq.astype(f32), mem_k.astype(f32)) * (X_HEAD_DIM ** -0.5)
    p = jax.nn.softmax(s, axis=-1)
    ctx = jnp.einsum('bhts,bshd->bthd', p, mem_v.astype(f32)).reshape(B, T, D_MODEL).astype(x.dtype)
    x = x + ctx @ w_xo
    hm = rmsnorm(x, norm_mlp_w) @ w_mlp_up
    x = x + jnp.square(jax.nn.relu(hm)) @ w_mlp_down
    return x, new_gdn_conv_buf, new_S.astype(gdn_S.dtype), new_sc_buf


def setup_inputs(seed: int = 0) -> dict:
    key = jax.random.key(seed)
    ks = iter(jax.random.split(key, 40))
    nk = lambda: next(ks)
    L = DEPTH
    w = lambda shape, fan_in: jax.random.normal(nk(), shape, jnp.float32) * (fan_in ** -0.5)
    gain = lambda shape: 1.0 + 0.02 * jax.random.normal(nk(), shape, jnp.float32)
    dt = jax.random.uniform(nk(), (L, GDN_HEADS), jnp.float32, minval=0.001, maxval=0.1)
    return {
        'x_prompt': jax.random.normal(nk(), (BATCH, SEQ, D_MODEL), jnp.float32),
        'x_sample': jax.random.normal(nk(), (DEC_BATCH, DEC_SEQ, D_MODEL), jnp.float32),
        'mem_prompt': jax.random.normal(nk(), (BATCH, N_MEM, D_MODEL), jnp.float32),
        'cache_mem_k': jax.random.normal(nk(), (L, DEC_BATCH, N_MEM, X_HEADS, X_HEAD_DIM), jnp.float32),
        'cache_mem_v': jax.random.normal(nk(), (L, DEC_BATCH, N_MEM, X_HEADS, X_HEAD_DIM), jnp.float32),
        'state_gdn_conv': jax.random.normal(nk(), (L, DEC_BATCH, GDN_CONV - 1, GDN_CONV_CH), jnp.float32),
        'state_gdn': 0.1 * jax.random.normal(nk(), (L, DEC_BATCH, GDN_HEADS, GDN_DK, GDN_DV), jnp.float32),
        'state_sc_conv': jax.random.normal(nk(), (L, DEC_BATCH, SC_CONV - 1, SC_W), jnp.float32),
        'norm_mix_w': gain((L, D_MODEL)),
        'w_in': w((L, D_MODEL, IN_COLS), D_MODEL),
        'gdn_conv_w': w((L, GDN_CONV, GDN_CONV_CH), GDN_CONV),
        'gdn_A_log': jnp.log(jax.random.uniform(nk(), (L, GDN_HEADS), jnp.float32, minval=1.0, maxval=16.0)),
        'gdn_dt_bias': jnp.log(jnp.expm1(dt)),
        'gdn_out_norm_w': gain((L, GDN_DV)),
        'w_gdn_o': w((L, GDN_V_W, D_MODEL), GDN_V_W),
        'sc_conv_w': w((L, SC_CONV, SC_W), SC_CONV),
        'w_sc_o': w((L, SC_W, D_MODEL), SC_W),
        'w_mix_out': w((L, D_MODEL, D_MODEL), D_MODEL),
        'norm_x_w': gain((L, D_MODEL)),
        'mem_norm_w': gain((L, D_MODEL)),
        'w_xq': w((L, D_MODEL, D_MODEL), D_MODEL),
        'w_xkv': w((L, D_MODEL, 2 * D_MODEL), D_MODEL),
        'w_xo': w((L, D_MODEL, D_MODEL), D_MODEL),
        'norm_mlp_w': gain((L, D_MODEL)),
        'w_mlp_up': w((L, D_MODEL, D_FF), D_MODEL),
        'w_mlp_down': w((L, D_FF, D_MODEL), D_FF),
        'norm_f_w': gain((D_MODEL,)),
    }


def reference(x_prompt, x_sample, mem_prompt, cache_mem_k, cache_mem_v, state_gdn_conv, state_gdn, state_sc_conv,
              norm_mix_w, w_in, gdn_conv_w, gdn_A_log, gdn_dt_bias, gdn_out_norm_w, w_gdn_o, sc_conv_w, w_sc_o,
              w_mix_out, norm_x_w, mem_norm_w, w_xq, w_xkv, w_xo, norm_mlp_w, w_mlp_up, w_mlp_down, norm_f_w):
    hp, hs = x_prompt, x_sample
    p_mk, p_mv, p_conv, p_S, p_sc = [], [], [], [], []
    s_conv, s_S, s_sc = [], [], []
    for i in range(DEPTH):
        lw = (norm_mix_w[i], w_in[i], gdn_conv_w[i], gdn_A_log[i], gdn_dt_bias[i], gdn_out_norm_w[i], w_gdn_o[i],
              sc_conv_w[i], w_sc_o[i], w_mix_out[i], norm_x_w[i], w_xq[i], w_xo[i], norm_mlp_w[i],
              w_mlp_up[i], w_mlp_down[i])
        mk, mv = memory_kv(mem_prompt, mem_norm_w[i], w_xkv[i])
        zero_conv = jnp.zeros((BATCH, GDN_CONV - 1, GDN_CONV_CH), hp.dtype)
        zero_S = jnp.zeros((BATCH, GDN_HEADS, GDN_DK, GDN_DV), state_gdn.dtype)
        zero_sc = jnp.zeros((BATCH, SC_CONV - 1, SC_W), hp.dtype)
        hp, pc, pS, psc = hybrid_layer(hp, mk, mv, zero_conv, zero_S, zero_sc, *lw)
        p_mk.append(mk)
        p_mv.append(mv)
        p_conv.append(pc)
        p_S.append(pS)
        p_sc.append(psc)
        hs, sc, sS, ssc = hybrid_layer(hs, cache_mem_k[i], cache_mem_v[i], state_gdn_conv[i], state_gdn[i],
                                       state_sc_conv[i], *lw)
        s_conv.append(sc)
        s_S.append(sS)
        s_sc.append(ssc)
    y_prompt = rmsnorm(hp, norm_f_w)
    y_sample = rmsnorm(hs, norm_f_w)
    return (y_prompt, y_sample,
            jnp.stack(p_mk), jnp.stack(p_mv), jnp.stack(p_conv), jnp.stack(p_S), jnp.stack(p_sc),
            jnp.stack(s_conv), jnp.stack(s_S), jnp.stack(s_sc))
```

```python
import functools

import jax
import jax.numpy as jnp
from jax import lax
from jax.experimental import pallas as pl
from jax.experimental.pallas import tpu as pltpu

F32 = jnp.float32
BF16 = jnp.bfloat16

D_MODEL = 1024
GDN_HEADS = 8
HEAD_DIM = 128
GDN_CHUNK = 64
GDN_CONV_W = 4
SC_CONV_W = 3
X_HEADS = 4
X_HEAD_DIM = 256
N_MEM = 256
D_FF = 4096
EPS = 1e-6
SUBLANES = 8
VMEM_LIMIT = 48 * 1024 * 1024

COL_Z = 3 * D_MODEL
COL_BG = 4 * D_MODEL
COL_CG = 5 * D_MODEL
COL_XIN = 6 * D_MODEL
COL_GA = 7 * D_MODEL
COL_GB = 8 * D_MODEL
MAIN_COLS = 9 * D_MODEL


def _params(n_axes):
    return pltpu.CompilerParams(dimension_semantics=("arbitrary",) * n_axes, vmem_limit_bytes=VMEM_LIMIT)


def _dot(a, b):
    return jnp.dot(a.astype(BF16), b.astype(BF16), preferred_element_type=F32)


def _dot_nt(a, b):
    return lax.dot_general(a.astype(BF16), b.astype(BF16), (((1,), (1,)), ((), ())), preferred_element_type=F32)


def _dot_tn(a, b):
    return lax.dot_general(a.astype(BF16), b.astype(BF16), (((0,), (0,)), ((), ())), preferred_element_type=F32)


def _split_hi_lo(x):
    hi = x.astype(BF16)
    lo = (x - hi.astype(F32)).astype(BF16)
    return hi, lo


def _dot_exact_lhs(m01, x):
    hi, lo = _split_hi_lo(x)
    m = m01.astype(BF16)
    return jnp.dot(m, hi, preferred_element_type=F32) + jnp.dot(m, lo, preferred_element_type=F32)


def _rms(x, w):
    return x * lax.rsqrt(jnp.mean(x * x, axis=-1, keepdims=True) + EPS) * w


def _sigmoid(x):
    return 1.0 / (1.0 + jnp.exp(-x))


def _silu(x):
    return x * _sigmoid(x)


def _softplus(x):
    return jnp.maximum(x, 0.0) + jnp.log(1.0 + jnp.exp(-jnp.abs(x)))


def _norm_kernel(x_ref, w_ref, o_ref):
    o_ref[...] = _rms(x_ref[...], w_ref[...]).astype(o_ref.dtype)


def rmsnorm_rows(x, w, tm):
    rows, d = x.shape
    return pl.pallas_call(
        _norm_kernel,
        grid=(rows // tm,),
        in_specs=[pl.BlockSpec((tm, d), lambda i: (i, 0)), pl.BlockSpec((1, d), lambda i: (0, 0))],
        out_specs=pl.BlockSpec((tm, d), lambda i: (i, 0)),
        out_shape=jax.ShapeDtypeStruct((rows, d), BF16),
        compiler_params=_params(1),
        name="rmsnorm_rows",
    )(x, w.reshape(1, d))


def _norm_gates_kernel(x_ref, w_ref, wab_ref, alog_ref, dtb_ref, o_ref, gb_ref):
    xn = _rms(x_ref[...], w_ref[...]).astype(BF16)
    o_ref[...] = xn
    ab = jnp.dot(xn, wab_ref[...], preferred_element_type=F32)
    a = ab[:, :GDN_HEADS]
    b = ab[:, GDN_HEADS:2 * GDN_HEADS]
    g = -jnp.exp(alog_ref[...]) * _softplus(a + dtb_ref[...])
    gb_ref[...] = jnp.concatenate([g, _sigmoid(b)], axis=1)


def rmsnorm_gates_rows(x, w, w_ab, a_log, dt_bias, tm):
    rows, d = x.shape
    wab = jnp.zeros((d, 128), BF16).at[:, :2 * GDN_HEADS].set(w_ab.astype(BF16))
    return pl.pallas_call(
        _norm_gates_kernel,
        grid=(rows // tm,),
        in_specs=[
            pl.BlockSpec((tm, d), lambda i: (i, 0)),
            pl.BlockSpec((1, d), lambda i: (0, 0)),
            pl.BlockSpec((d, 128), lambda i: (0, 0)),
            pl.BlockSpec((1, GDN_HEADS), lambda i: (0, 0)),
            pl.BlockSpec((1, GDN_HEADS), lambda i: (0, 0)),
        ],
        out_specs=[pl.BlockSpec((tm, d), lambda i: (i, 0)), pl.BlockSpec((tm, 2 * GDN_HEADS), lambda i: (i, 0))],
        out_shape=[jax.ShapeDtypeStruct((rows, d), BF16), jax.ShapeDtypeStruct((rows, 2 * GDN_HEADS), F32)],
        compiler_params=_params(1),
        name="rmsnorm_gates_rows",
    )(x, w.reshape(1, d), wab, a_log.reshape(1, GDN_HEADS), dt_bias.reshape(1, GDN_HEADS))


def _mm_kernel(a_ref, w_ref, o_ref, *, relu2):
    acc = _dot(a_ref[...], w_ref[...])
    if relu2:
        acc = jnp.square(jnp.maximum(acc, 0.0))
    o_ref[...] = acc.astype(o_ref.dtype)


def matmul_rows(a, w, out_dtype, tm, tn, relu2=False):
    rows, k = a.shape
    n = w.shape[1]
    return pl.pallas_call(
        functools.partial(_mm_kernel, relu2=relu2),
        grid=(n // tn, rows // tm),
        in_specs=[pl.BlockSpec((tm, k), lambda j, i: (i, 0)), pl.BlockSpec((k, tn), lambda j, i: (0, j))],
        out_specs=pl.BlockSpec((tm, tn), lambda j, i: (i, j)),
        out_shape=jax.ShapeDtypeStruct((rows, n), out_dtype),
        compiler_params=_params(2),
        name="matmul_rows",
    )(a, w)


def _mm_res_norm_kernel(a_ref, w_ref, res_ref, nw_ref, x_ref, xn_ref):
    x = res_ref[...] + _dot(a_ref[...], w_ref[...])
    x_ref[...] = x
    xn_ref[...] = _rms(x, nw_ref[...]).astype(xn_ref.dtype)


def matmul_residual_norm(a, w, res, norm_w, xn_dtype, tm):
    rows, k = a.shape
    n = w.shape[1]
    return pl.pallas_call(
        _mm_res_norm_kernel,
        grid=(rows // tm,),
        in_specs=[
            pl.BlockSpec((tm, k), lambda i: (i, 0)),
            pl.BlockSpec((k, n), lambda i: (0, 0)),
            pl.BlockSpec((tm, n), lambda i: (i, 0)),
            pl.BlockSpec((1, n), lambda i: (0, 0)),
        ],
        out_specs=[pl.BlockSpec((tm, n), lambda i: (i, 0)), pl.BlockSpec((tm, n), lambda i: (i, 0))],
        out_shape=[jax.ShapeDtypeStruct((rows, n), F32), jax.ShapeDtypeStruct((rows, n), xn_dtype)],
        compiler_params=_params(1),
        name="matmul_residual_norm",
    )(a, w, res, norm_w.reshape(1, n))


def _merge_kernel(oa_ref, ub_ref, ga_ref, gb_ref, x_ref, wa_ref, wb_ref, wm_ref, nw_ref, x1_ref, xn_ref):
    ya = _dot(oa_ref[...], wa_ref[...])
    yb = _dot(ub_ref[...], wb_ref[...])
    merged = _sigmoid(ga_ref[...].astype(F32)) * ya + _sigmoid(gb_ref[...].astype(F32)) * yb
    x1 = x_ref[...] + _dot(merged, wm_ref[...])
    x1_ref[...] = x1
    xn_ref[...] = _rms(x1, nw_ref[...]).astype(xn_ref.dtype)


def merge_mixers(o_a, u_b, proj, x, w_gdn_o, w_sc_o, w_mix_out, norm_w, tm):
    rows, d = x.shape
    row_blk = lambda i: (i, 0)
    full = lambda i: (0, 0)
    return pl.pallas_call(
        _merge_kernel,
        grid=(rows // tm,),
        in_specs=[
            pl.BlockSpec((tm, d), row_blk),
            pl.BlockSpec((tm, d), row_blk),
            pl.BlockSpec((tm, d), lambda i: (i, COL_GA // D_MODEL)),
            pl.BlockSpec((tm, d), lambda i: (i, COL_GB // D_MODEL)),
            pl.BlockSpec((tm, d), row_blk),
            pl.BlockSpec((d, d), full),
            pl.BlockSpec((d, d), full),
            pl.BlockSpec((d, d), full),
            pl.BlockSpec((1, d), full),
        ],
        out_specs=[pl.BlockSpec((tm, d), row_blk), pl.BlockSpec((tm, d), row_blk)],
        out_shape=[jax.ShapeDtypeStruct((rows, d), F32), jax.ShapeDtypeStruct((rows, d), BF16)],
        compiler_params=_params(1),
        name="merge_mixers",
    )(o_a, u_b, proj, proj, x, w_gdn_o, w_sc_o, w_mix_out, norm_w.reshape(1, d))


def _mm_res_final_kernel(a_ref, w_ref, res_ref, nw_ref, y_ref):
    x = res_ref[...] + _dot(a_ref[...], w_ref[...])
    y_ref[...] = _rms(x, nw_ref[...])


def matmul_residual_final_norm(a, w, res, norm_w, tm):
    rows, k = a.shape
    n = w.shape[1]
    return pl.pallas_call(
        _mm_res_final_kernel,
        grid=(rows // tm,),
        in_specs=[
            pl.BlockSpec((tm, k), lambda i: (i, 0)),
            pl.BlockSpec((k, n), lambda i: (0, 0)),
            pl.BlockSpec((tm, n), lambda i: (i, 0)),
            pl.BlockSpec((1, n), lambda i: (0, 0)),
        ],
        out_specs=pl.BlockSpec((tm, n), lambda i: (i, 0)),
        out_shape=jax.ShapeDtypeStruct((rows, n), F32),
        compiler_params=_params(1),
        name="matmul_residual_final_norm",
    )(a, w, res, norm_w.reshape(1, n))


def _shifted(x, prev_tail, s):
    rolled = pltpu.roll(x, s, 0)
    head = pltpu.roll(prev_tail, s, 0)
    row = lax.broadcasted_iota(jnp.int32, head.shape, 0)
    first = jnp.where(row < s, head, rolled[:SUBLANES])
    if x.shape[0] == SUBLANES:
        return first
    return jnp.concatenate([first, rolled[SUBLANES:]], axis=0)


def _causal_conv(x, prev_tail, w):
    width = w.shape[0]
    y = x * w[width - 1:width]
    for i in range(width - 1):
        y = y + _shifted(x, prev_tail, width - 1 - i) * w[i:i + 1]
    return y


def _l2norm_heads(x, scale):
    outs = []
    for h in range(GDN_HEADS):
        xh = x[:, h * HEAD_DIM:(h + 1) * HEAD_DIM]
        outs.append(xh * (lax.rsqrt(jnp.sum(xh * xh, axis=-1, keepdims=True) + EPS) * scale))
    return jnp.concatenate(outs, axis=1)


def _prep_body(qkv, bg, cg, xin, qkv_tail, cx_tail, cw, sw):
    conv = _silu(_causal_conv(qkv, qkv_tail, cw))
    q = _l2norm_heads(conv[:, :D_MODEL], HEAD_DIM ** -0.5)
    k = _l2norm_heads(conv[:, D_MODEL:2 * D_MODEL], 1.0)
    v = conv[:, 2 * D_MODEL:]
    cx = cg * xin
    u = bg * _causal_conv(cx, cx_tail, sw)
    return q, k, v, u, cx


def _prep_prompt_kernel(qkv_ref, bg_ref, cg_ref, xin_ref, cw_ref, sw_ref,
                        q_ref, k_ref, v_ref, u_ref, cxt_ref, qkv_tail, cx_tail):
    @pl.when(pl.program_id(1) == 0)
    def _():
        qkv_tail[...] = jnp.zeros_like(qkv_tail)
        cx_tail[...] = jnp.zeros_like(cx_tail)

    qkv = qkv_ref[...].astype(F32)
    q, k, v, u, cx = _prep_body(qkv, bg_ref[...].astype(F32), cg_ref[...].astype(F32), xin_ref[...].astype(F32),
                                qkv_tail[...], cx_tail[...], cw_ref[...], sw_ref[...])
    q_ref[...] = q.astype(q_ref.dtype)
    k_ref[...] = k.astype(k_ref.dtype)
    v_ref[...] = v.astype(v_ref.dtype)
    u_ref[...] = u.astype(u_ref.dtype)
    tm = qkv.shape[0]
    qkv_tail[...] = qkv[tm - SUBLANES:]
    cx_tail[...] = cx[tm - SUBLANES:]
    cxt_ref[0] = cx[tm - SUBLANES:]


def prep_prompt(proj, conv_w, sc_w, batch, seq, tm):
    rows = batch * seq
    nt = seq // tm
    d = D_MODEL
    blk = lambda c: pl.BlockSpec((tm, d), lambda b, t, c=c: (b * nt + t, c))
    out_rows = pl.BlockSpec((tm, d), lambda b, t: (b * nt + t, 0))
    return pl.pallas_call(
        _prep_prompt_kernel,
        grid=(batch, nt),
        in_specs=[
            pl.BlockSpec((tm, 3 * d), lambda b, t: (b * nt + t, 0)),
            blk(COL_BG // d), blk(COL_CG // d), blk(COL_XIN // d),
            pl.BlockSpec((GDN_CONV_W, 3 * d), lambda b, t: (0, 0)),
            pl.BlockSpec((SC_CONV_W, d), lambda b, t: (0, 0)),
        ],
        out_specs=[out_rows, out_rows, out_rows, out_rows, pl.BlockSpec((1, SUBLANES, d), lambda b, t: (b, 0, 0))],
        out_shape=[jax.ShapeDtypeStruct((rows, d), BF16)] * 4 + [jax.ShapeDtypeStruct((batch, SUBLANES, d), F32)],
        scratch_shapes=[pltpu.VMEM((SUBLANES, 3 * d), F32), pltpu.VMEM((SUBLANES, d), F32)],
        compiler_params=_params(2),
        name="prep_prompt",
    )(proj, proj, proj, proj, conv_w, sc_w)


def _prep_sample_kernel(qkv_ref, bg_ref, cg_ref, xin_ref, qst_ref, cst_ref, cw_ref, sw_ref,
                        q_ref, k_ref, v_ref, u_ref, cx_ref):
    q, k, v, u, cx = _prep_body(qkv_ref[...], bg_ref[...], cg_ref[...], xin_ref[...],
                                qst_ref[0], cst_ref[0], cw_ref[...], sw_ref[...])
    q_ref[...] = q
    k_ref[...] = k
    v_ref[...] = v
    u_ref[...] = u
    cx_ref[...] = cx


def prep_sample(proj, qkv_state8, cx_state8, conv_w, sc_w, n_seq):
    rows = n_seq * SUBLANES
    d = D_MODEL
    t = SUBLANES
    blk = lambda c: pl.BlockSpec((t, d), lambda n, c=c: (n, c))
    out_rows = pl.BlockSpec((t, d), lambda n: (n, 0))
    return pl.pallas_call(
        _prep_sample_kernel,
        grid=(n_seq,),
        in_specs=[
            pl.BlockSpec((t, 3 * d), lambda n: (n, 0)),
            blk(COL_BG // d), blk(COL_CG // d), blk(COL_XIN // d),
            pl.BlockSpec((1, t, 3 * d), lambda n: (n, 0, 0)),
            pl.BlockSpec((1, t, d), lambda n: (n, 0, 0)),
            pl.BlockSpec((GDN_CONV_W, 3 * d), lambda n: (0, 0)),
            pl.BlockSpec((SC_CONV_W, d), lambda n: (0, 0)),
        ],
        out_specs=[out_rows] * 5,
        out_shape=[jax.ShapeDtypeStruct((rows, d), F32)] * 5,
        compiler_params=_params(1),
        name="prep_sample",
    )(proj, proj, proj, proj, qkv_state8, cx_state8, conv_w, sc_w)


def _tri_inverse(l_mat, n, top):
    ri = lax.broadcasted_iota(jnp.int32, (n, n), 0)
    ci = lax.broadcasted_iota(jnp.int32, (n, n), 1)
    t = jnp.where(ri == ci, 1.0, 0.0) - jnp.where((ri ^ ci) == 1, l_mat, 0.0)
    shift = 1
    while (1 << shift) < top:
        a21 = jnp.where(((ri >> shift) ^ (ci >> shift)) == 1, l_mat, 0.0)
        t = t - _dot(t, _dot(a21, t))
        shift += 1
    return t


def _pair_diff(col_b):
    hi, lo = _split_hi_lo(col_b)
    hi = hi.astype(F32)
    lo = lo.astype(F32)
    lane = lax.broadcasted_iota(jnp.int32, col_b.shape, 1)
    left = jnp.where(lane == 0, hi, jnp.where(lane == 1, lo, jnp.where(lane < 4, 1.0, 0.0)))
    right = jnp.where(lane == 2, -hi, jnp.where(lane == 3, -lo, jnp.where(lane < 2, 1.0, 0.0)))
    return _dot_nt(left, right)


def _gdn_intra(q, k, v, gcum_b, beta_b, block):
    n = q.shape[0]
    ri = lax.broadcasted_iota(jnp.int32, (n, n), 0)
    ci = lax.broadcasted_iota(jnp.int32, (n, n), 1)
    same = (ri >> (block.bit_length() - 1)) == (ci >> (block.bit_length() - 1))
    diff = _pair_diff(gcum_b)
    decay = jnp.where(same, jnp.where(ci <= ri, jnp.exp(jnp.minimum(diff, 0.0)), 0.0), 0.0)
    strict = jnp.where(ci < ri, decay, 0.0)
    l_mat = beta_b[:, :n] * _dot_nt(k, k) * strict
    t = _tri_inverse(l_mat, n, block)
    egc = jnp.exp(gcum_b)
    wu = _dot(t, jnp.concatenate([k * (beta_b * egc), v * beta_b], axis=1))
    p = _dot_nt(q, k) * decay
    return wu[:, :HEAD_DIM], wu[:, HEAD_DIM:], q * egc, p


def _gated_out(o, z, onw):
    return _rms(o, onw) * _silu(z)


def _gdn_prompt_kernel(q_ref, k_ref, v_ref, z_ref, gb_ref, onw_ref, o_ref, s_out_ref, s_ref):
    c = pl.program_id(1)

    @pl.when(c == 0)
    def _():
        s_ref[...] = jnp.zeros_like(s_ref)

    n = GDN_CHUNK
    gb = gb_ref[...]
    ri = lax.broadcasted_iota(jnp.int32, (n, n), 0)
    ci = lax.broadcasted_iota(jnp.int32, (n, n), 1)
    gcum = _dot_exact_lhs(jnp.where(ci <= ri, 1.0, 0.0), gb)
    for h in range(GDN_HEADS):
        sl = slice(h * HEAD_DIM, (h + 1) * HEAD_DIM)
        q = q_ref[:, sl].astype(F32)
        k = k_ref[:, sl].astype(F32)
        v = v_ref[:, sl].astype(F32)
        gcum_b = jnp.broadcast_to(gcum[:, h:h + 1], (n, HEAD_DIM))
        beta_b = jnp.broadcast_to(gb[:, GDN_HEADS + h:GDN_HEADS + h + 1], (n, HEAD_DIM))
        w, uv, qe, p = _gdn_intra(q, k, v, gcum_b, beta_b, n)
        s = s_ref[h]
        wq = _dot(jnp.concatenate([w, qe], axis=0), s)
        u = uv - wq[:n]
        o = wq[n:] + _dot(p, u)
        glast = gcum_b[n - 1:n, :]
        ke = k * jnp.exp(glast - gcum_b)
        s_ref[h] = jnp.exp(glast) * s + _dot_tn(ke, u)
        o_ref[:, sl] = _gated_out(o, z_ref[:, sl].astype(F32), onw_ref[...]).astype(o_ref.dtype)

    @pl.when(c == pl.num_programs(1) - 1)
    def _():
        s_out_ref[0] = s_ref[...]


def gdn_prompt(q, k, v, proj, gb, out_norm_w, batch, seq):
    rows = batch * seq
    n = GDN_CHUNK
    nc = seq // n
    d = D_MODEL
    rowblk = pl.BlockSpec((n, d), lambda b, c: (b * nc + c, 0))
    return pl.pallas_call(
        _gdn_prompt_kernel,
        grid=(batch, nc),
        in_specs=[
            rowblk, rowblk, rowblk,
            pl.BlockSpec((n, d), lambda b, c: (b * nc + c, COL_Z // d)),
            pl.BlockSpec((n, 2 * GDN_HEADS), lambda b, c: (b * nc + c, 0)),
            pl.BlockSpec((1, HEAD_DIM), lambda b, c: (0, 0)),
        ],
        out_specs=[rowblk, pl.BlockSpec((1, GDN_HEADS, HEAD_DIM, HEAD_DIM), lambda b, c: (b, 0, 0, 0))],
        out_shape=[jax.ShapeDtypeStruct((rows, d), BF16),
                   jax.ShapeDtypeStruct((batch, GDN_HEADS, HEAD_DIM, HEAD_DIM), F32)],
        scratch_shapes=[pltpu.VMEM((GDN_HEADS, HEAD_DIM, HEAD_DIM), F32)],
        compiler_params=_params(2),
        name="gdn_prompt",
    )(q, k, v, proj, gb, out_norm_w.reshape(1, HEAD_DIM))


def _stack_heads(x):
    return jnp.concatenate([x[:, h * HEAD_DIM:(h + 1) * HEAD_DIM] for h in range(GDN_HEADS)], axis=0)


def _gdn_sample_kernel(q_ref, k_ref, v_ref, z_ref, gb_ref, onw_ref, s_in_ref, o_ref, s_out_ref):
    t = SUBLANES
    n = GDN_HEADS * t
    q = _stack_heads(q_ref[...])
    k = _stack_heads(k_ref[...])
    v = _stack_heads(v_ref[...])
    z = _stack_heads(z_ref[...])
    gb = gb_ref[...]
    g_b = jnp.concatenate([jnp.broadcast_to(gb[:, h:h + 1], (t, HEAD_DIM)) for h in range(GDN_HEADS)], axis=0)
    beta_b = jnp.concatenate(
        [jnp.broadcast_to(gb[:, GDN_HEADS + h:GDN_HEADS + h + 1], (t, HEAD_DIM)) for h in range(GDN_HEADS)], axis=0)
    ri = lax.broadcasted_iota(jnp.int32, (n, n), 0)
    ci = lax.broadcasted_iota(jnp.int32, (n, n), 1)
    cum_mask = jnp.where(((ri >> 3) == (ci >> 3)), jnp.where(ci <= ri, 1.0, 0.0), 0.0)
    gcum_b = _dot_exact_lhs(cum_mask, g_b)
    w, uv, qe, p = _gdn_intra(q, k, v, gcum_b, beta_b, t)
    glast_b = jnp.concatenate(
        [jnp.broadcast_to(gcum_b[h * t + t - 1:h * t + t, :], (t, HEAD_DIM)) for h in range(GDN_HEADS)], axis=0)
    ke = k * jnp.exp(glast_b - gcum_b)
    wqe = jnp.concatenate([w, qe], axis=0)
    ws_rows = []
    qs_rows = []
    for h in range(GDN_HEADS):
        wq = _dot(wqe, s_in_ref[0, h])
        ws_rows.append(wq[h * t:(h + 1) * t])
        qs_rows.append(wq[n + h * t:n + (h + 1) * t])
    u = uv - jnp.concatenate(ws_rows, axis=0)
    o = jnp.concatenate(qs_rows, axis=0) + _dot(p, u)
    row = lax.broadcasted_iota(jnp.int32, (n, HEAD_DIM), 0)
    for h in range(GDN_HEADS):
        u_h = jnp.where((row >> 3) == h, u, 0.0)
        s_out_ref[0, h] = jnp.exp(glast_b[h * t:h * t + 1, :]) * s_in_ref[0, h] + _dot_tn(ke, u_h)
    og = _gated_out(o, z, onw_ref[...])
    for h in range(GDN_HEADS):
        o_ref[:, h * HEAD_DIM:(h + 1) * HEAD_DIM] = og[h * t:(h + 1) * t]


def gdn_sample(q, k, v, proj, gb, out_norm_w, state, n_seq):
    t = SUBLANES
    d = D_MODEL
    rowblk = pl.BlockSpec((t, d), lambda i: (i, 0))
    sblk = pl.BlockSpec((1, GDN_HEADS, HEAD_DIM, HEAD_DIM), lambda i: (i, 0, 0, 0))
    return pl.pallas_call(
        _gdn_sample_kernel,
        grid=(n_seq,),
        in_specs=[
            rowblk, rowblk, rowblk,
            pl.BlockSpec((t, d), lambda i: (i, COL_Z // d)),
            pl.BlockSpec((t, 2 * GDN_HEADS), lambda i: (i, 0)),
            pl.BlockSpec((1, HEAD_DIM), lambda i: (0, 0)),
            sblk,
        ],
        out_specs=[rowblk, sblk],
        out_shape=[jax.ShapeDtypeStruct((n_seq * t, d), F32),
                   jax.ShapeDtypeStruct((n_seq, GDN_HEADS, HEAD_DIM, HEAD_DIM), F32)],
        compiler_params=_params(1),
        name="gdn_sample",
    )(q, k, v, proj, gb, out_norm_w.reshape(1, HEAD_DIM), state)


def _attend(q, mk_ref, mv_ref, i):
    outs = []
    for h in range(X_HEADS):
        sl = slice(h * X_HEAD_DIM, (h + 1) * X_HEAD_DIM)
        s = _dot_nt(q[:, sl], mk_ref[i, :, sl]) * (X_HEAD_DIM ** -0.5)
        e = jnp.exp(s - jnp.max(s, axis=-1, keepdims=True))
        p = e / jnp.sum(e, axis=-1, keepdims=True)
        outs.append(_dot(p, mv_ref[i, :, sl]))
    return jnp.concatenate(outs, axis=1)


def _attn_prompt_kernel(q_ref, mk_ref, mv_ref, o_ref):
    o_ref[...] = _attend(q_ref[...], mk_ref, mv_ref, 0).astype(o_ref.dtype)


def attention_prompt(q, mkv, batch, seq, tm):
    nt = seq // tm
    d = D_MODEL
    return pl.pallas_call(
        _attn_prompt_kernel,
        grid=(batch, nt),
        in_specs=[
            pl.BlockSpec((tm, d), lambda b, t: (b * nt + t, 0)),
            pl.BlockSpec((1, N_MEM, d), lambda b, t: (b, 0, 0)),
            pl.BlockSpec((1, N_MEM, d), lambda b, t: (b, 0, 1)),
        ],
        out_specs=pl.BlockSpec((tm, d), lambda b, t: (b * nt + t, 0)),
        out_shape=jax.ShapeDtypeStruct((batch * seq, d), BF16),
        compiler_params=_params(2),
        name="attention_prompt",
    )(q, mkv, mkv)


def _attn_sample_kernel(q_ref, mk_ref, mv_ref, o_ref, *, n_seq):
    t = SUBLANES
    for i in range(n_seq):
        q = q_ref[i * t:(i + 1) * t, :]
        q16 = jnp.concatenate([q, jnp.zeros_like(q)], axis=0)
        o_ref[i * t:(i + 1) * t, :] = _attend(q16, mk_ref, mv_ref, i)[:t]


def attention_sample(q, mem_k, mem_v, n_seq, per_step):
    t = SUBLANES
    d = D_MODEL
    return pl.pallas_call(
        functools.partial(_attn_sample_kernel, n_seq=per_step),
        grid=(n_seq // per_step,),
        in_specs=[
            pl.BlockSpec((per_step * t, d), lambda i: (i, 0)),
            pl.BlockSpec((per_step, N_MEM, d), lambda i: (i, 0, 0)),
            pl.BlockSpec((per_step, N_MEM, d), lambda i: (i, 0, 0)),
        ],
        out_specs=pl.BlockSpec((per_step * t, d), lambda i: (i, 0)),
        out_shape=jax.ShapeDtypeStruct((n_seq * t, d), F32),
        compiler_params=_params(1),
        name="attention_sample",
    )(q, mem_k, mem_v)


def _layer_tail(x1, xn1, ctx_fn, w, tm, tn):
    hq = matmul_rows(xn1, w["xq"], xn1.dtype, tm, tn)
    ctx = ctx_fn(hq)
    x2, xn2 = matmul_residual_norm(ctx, w["xo"], x1, w["norm_mlp"], BF16, tm)
    hm = matmul_rows(xn2, w["up"], BF16, tm, tn, relu2=True)
    return matmul_residual_final_norm(hm, w["down"], x2, w["norm_f"], tm)


def kernel(x_prompt, x_sample, mem_prompt, cache_mem_k, cache_mem_v, state_gdn_conv, state_gdn, state_sc_conv, norm_mix_w, w_in, gdn_conv_w, gdn_A_log, gdn_dt_bias, gdn_out_norm_w, w_gdn_o, sc_conv_w, w_sc_o, w_mix_out, norm_x_w, mem_norm_w, w_xq, w_xkv, w_xo, norm_mlp_w, w_mlp_up, w_mlp_down, norm_f_w):
    batch, seq, d = x_prompt.shape
    n_seq, t_dec, _ = x_sample.shape
    assert norm_mix_w.shape[0] == 1 and t_dec == SUBLANES and d == D_MODEL
    ab0 = 4 * D_MODEL
    w_in0 = w_in[0]
    w = {
        "main": jnp.concatenate([w_in0[:, :ab0], w_in0[:, ab0 + 2 * GDN_HEADS:]], axis=1).astype(BF16),
        "ab": w_in0[:, ab0:ab0 + 2 * GDN_HEADS],
        "gdn_o": w_gdn_o[0].astype(BF16), "sc_o": w_sc_o[0].astype(BF16), "mix": w_mix_out[0].astype(BF16),
        "xq": w_xq[0].astype(BF16), "xkv": w_xkv[0].astype(BF16), "xo": w_xo[0].astype(BF16),
        "up": w_mlp_up[0].astype(BF16), "down": w_mlp_down[0].astype(BF16),
        "norm_mlp": norm_mlp_w[0], "norm_f": norm_f_w,
    }
    tm, tn = 512, 1024

    rows_p = batch * seq
    xp = x_prompt.reshape(rows_p, d)
    memn = rmsnorm_rows(mem_prompt.reshape(batch * N_MEM, d), mem_norm_w[0], tm)
    mkv = matmul_rows(memn, w["xkv"], F32, tm, tn)
    xn, gb = rmsnorm_gates_rows(xp, norm_mix_w[0], w["ab"], gdn_A_log[0], gdn_dt_bias[0], tm)
    proj = matmul_rows(xn, w["main"], BF16, tm, tn)
    q, k, v, u_sc, cx_tail = prep_prompt(proj, gdn_conv_w[0], sc_conv_w[0], batch, seq, 256)
    o_gdn, s_prompt = gdn_prompt(q, k, v, proj, gb, gdn_out_norm_w[0], batch, seq)
    x1, xn1 = merge_mixers(o_gdn, u_sc, proj, xp, w["gdn_o"], w["sc_o"], w["mix"], norm_x_w[0], tm)
    mkv3 = mkv.reshape(batch, N_MEM, 2 * d)
    y_prompt = _layer_tail(x1, xn1, lambda hq: attention_prompt(hq, mkv3, batch, seq, tm), w, tm, tn)
    p_conv = proj.reshape(batch, seq, MAIN_COLS)[:, seq - (GDN_CONV_W - 1):, :3 * d].astype(F32)
    p_sc = cx_tail[:, SUBLANES - (SC_CONV_W - 1):]
    p_mk = mkv3[:, :, :d].reshape(batch, N_MEM, X_HEADS, X_HEAD_DIM)
    p_mv = mkv3[:, :, d:].reshape(batch, N_MEM, X_HEADS, X_HEAD_DIM)

    rows_s = n_seq * t_dec
    xs = x_sample.reshape(rows_s, d)
    xn_s, gb_s = rmsnorm_gates_rows(xs, norm_mix_w[0], w["ab"], gdn_A_log[0], gdn_dt_bias[0], tm)
    proj_s = matmul_rows(xn_s, w["main"], F32, tm, tn)
    qst8 = jnp.pad(state_gdn_conv[0], ((0, 0), (SUBLANES - (GDN_CONV_W - 1), 0), (0, 0)))
    cst8 = jnp.pad(state_sc_conv[0], ((0, 0), (SUBLANES - (SC_CONV_W - 1), 0), (0, 0)))
    q_s, k_s, v_s, u_s, cx_s = prep_sample(proj_s, qst8, cst8, gdn_conv_w[0], sc_conv_w[0], n_seq)
    o_s, s_sample = gdn_sample(q_s, k_s, v_s, proj_s, gb_s, gdn_out_norm_w[0], state_gdn[0], n_seq)
    x1_s, xn1_s = merge_mixers(o_s, u_s, proj_s, xs, w["gdn_o"], w["sc_o"], w["mix"], norm_x_w[0], tm)
    mk_s = cache_mem_k[0].reshape(n_seq, N_MEM, d)
    mv_s = cache_mem_v[0].reshape(n_seq, N_MEM, d)
    y_sample = _layer_tail(x1_s, xn1_s, lambda hq: attention_sample(hq.astype(F32), mk_s, mv_s, n_seq, 4), w, tm, tn)
    s_conv = proj_s.reshape(n_seq, t_dec, MAIN_COLS)[:, t_dec - (GDN_CONV_W - 1):, :3 * d]
    s_sc = cx_s.reshape(n_seq, t_dec, d)[:, t_dec - (SC_CONV_W - 1):]

    return (y_prompt.reshape(batch, seq, d), y_sample.reshape(n_seq, t_dec, d),
            p_mk[None], p_mv[None], p_conv[None], s_prompt[None], p_sc[None],
            s_conv[None], s_sample[None], s_sc[None])
```

```python
import functools

import jax
import jax.numpy as jnp
from jax import lax
from jax.experimental import pallas as pl
from jax.experimental.pallas import tpu as pltpu

F32 = jnp.float32
BF16 = jnp.bfloat16

D_MODEL = 1024
GDN_HEADS = 8
HEAD_DIM = 128
GDN_CHUNK = 64
GDN_CONV_W = 4
SC_CONV_W = 3
X_HEADS = 4
X_HEAD_DIM = 256
N_MEM = 256
D_FF = 4096
EPS = 1e-6
SUBLANES = 8
VMEM_LIMIT = 48 * 1024 * 1024

COL_Z = 3 * D_MODEL
COL_BG = 4 * D_MODEL
COL_CG = 5 * D_MODEL
COL_XIN = 6 * D_MODEL
COL_GA = 7 * D_MODEL
COL_GB = 8 * D_MODEL
MAIN_COLS = 9 * D_MODEL


def _params(n_axes):
    return pltpu.CompilerParams(dimension_semantics=("arbitrary",) * n_axes, vmem_limit_bytes=VMEM_LIMIT)


def _dot(a, b):
    return jnp.dot(a.astype(BF16), b.astype(BF16), preferred_element_type=F32)


def _dot_nt(a, b):
    return lax.dot_general(a.astype(BF16), b.astype(BF16), (((1,), (1,)), ((), ())), preferred_element_type=F32)


def _dot_tn(a, b):
    return lax.dot_general(a.astype(BF16), b.astype(BF16), (((0,), (0,)), ((), ())), preferred_element_type=F32)


def _split_hi_lo(x):
    hi = x.astype(BF16)
    lo = (x - hi.astype(F32)).astype(BF16)
    return hi, lo


def _dot_exact_lhs(m01, x):
    hi, lo = _split_hi_lo(x)
    m = m01.astype(BF16)
    return jnp.dot(m, hi, preferred_element_type=F32) + jnp.dot(m, lo, preferred_element_type=F32)


def _rms(x, w):
    return x * lax.rsqrt(jnp.mean(x * x, axis=-1, keepdims=True) + EPS) * w


def _sigmoid(x):
    return 1.0 / (1.0 + jnp.exp(-x))


def _silu(x):
    return x * _sigmoid(x)


def _softplus(x):
    return jnp.maximum(x, 0.0) + jnp.log(1.0 + jnp.exp(-jnp.abs(x)))


def _norm_kernel(x_ref, w_ref, o_ref):
    o_ref[...] = _rms(x_ref[...], w_ref[...]).astype(o_ref.dtype)


def rmsnorm_rows(x, w, tm):
    rows, d = x.shape
    return pl.pallas_call(
        _norm_kernel,
        grid=(rows // tm,),
        in_specs=[pl.BlockSpec((tm, d), lambda i: (i, 0)), pl.BlockSpec((1, d), lambda i: (0, 0))],
        out_specs=pl.BlockSpec((tm, d), lambda i: (i, 0)),
        out_shape=jax.ShapeDtypeStruct((rows, d), BF16),
        compiler_params=_params(1),
        name="rmsnorm_rows",
    )(x, w.reshape(1, d))


def _norm_gates_kernel(x_ref, w_ref, wab_ref, alog_ref, dtb_ref, o_ref, gb_ref):
    xn = _rms(x_ref[...], w_ref[...]).astype(BF16)
    o_ref[...] = xn
    ab = jnp.dot(xn, wab_ref[...], preferred_element_type=F32)
    a = ab[:, :GDN_HEADS]
    b = ab[:, GDN_HEADS:2 * GDN_HEADS]
    g = -jnp.exp(alog_ref[...]) * _softplus(a + dtb_ref[...])
    gb_ref[...] = jnp.concatenate([g, _sigmoid(b)], axis=1)


def rmsnorm_gates_rows(x, w, w_ab, a_log, dt_bias, tm):
    rows, d = x.shape
    wab = jnp.zeros((d, 128), BF16).at[:, :2 * GDN_HEADS].set(w_ab.astype(BF16))
    return pl.pallas_call(
        _norm_gates_kernel,
        grid=(rows // tm,),
        in_specs=[
            pl.BlockSpec((tm, d), lambda i: (i, 0)),
            pl.BlockSpec((1, d), lambda i: (0, 0)),
            pl.BlockSpec((d, 128), lambda i: (0, 0)),
            pl.BlockSpec((1, GDN_HEADS), lambda i: (0, 0)),
            pl.BlockSpec((1, GDN_HEADS), lambda i: (0, 0)),
        ],
        out_specs=[pl.BlockSpec((tm, d), lambda i: (i, 0)), pl.BlockSpec((tm, 2 * GDN_HEADS), lambda i: (i, 0))],
        out_shape=[jax.ShapeDtypeStruct((rows, d), BF16), jax.ShapeDtypeStruct((rows, 2 * GDN_HEADS), F32)],
        compiler_params=_params(1),
        name="rmsnorm_gates_rows",
    )(x, w.reshape(1, d), wab, a_log.reshape(1, GDN_HEADS), dt_bias.reshape(1, GDN_HEADS))


def _mm_kernel(a_ref, w_ref, o_ref, *, relu2):
    acc = _dot(a_ref[...], w_ref[...])
    if relu2:
        acc = jnp.square(jnp.maximum(acc, 0.0))
    o_ref[...] = acc.astype(o_ref.dtype)


def matmul_rows(a, w, out_dtype, tm, tn, relu2=False):
    rows, k = a.shape
    n = w.shape[1]
    return pl.pallas_call(
        functools.partial(_mm_kernel, relu2=relu2),
        grid=(n // tn, rows // tm),
        in_specs=[pl.BlockSpec((tm, k), lambda j, i: (i, 0)), pl.BlockSpec((k, tn), lambda j, i: (0, j))],
        out_specs=pl.BlockSpec((tm, tn), lambda j, i: (i, j)),
        out_shape=jax.ShapeDtypeStruct((rows, n), out_dtype),
        compiler_params=_params(2),
        name="matmul_rows",
    )(a, w)


def _mm_res_norm_kernel(a_ref, w_ref, res_ref, nw_ref, x_ref, xn_ref):
    x = res_ref[...] + _dot(a_ref[...], w_ref[...])
    x_ref[...] = x
    xn_ref[...] = _rms(x, nw_ref[...]).astype(xn_ref.dtype)


def matmul_residual_norm(a, w, res, norm_w, xn_dtype, tm):
    rows, k = a.shape
    n = w.shape[1]
    return pl.pallas_call(
        _mm_res_norm_kernel,
        grid=(rows // tm,),
        in_specs=[
            pl.BlockSpec((tm, k), lambda i: (i, 0)),
            pl.BlockSpec((k, n), lambda i: (0, 0)),
            pl.BlockSpec((tm, n), lambda i: (i, 0)),
            pl.BlockSpec((1, n), lambda i: (0, 0)),
        ],
        out_specs=[pl.BlockSpec((tm, n), lambda i: (i, 0)), pl.BlockSpec((tm, n), lambda i: (i, 0))],
        out_shape=[jax.ShapeDtypeStruct((rows, n), F32), jax.ShapeDtypeStruct((rows, n), xn_dtype)],
        compiler_params=_params(1),
        name="matmul_residual_norm",
    )(a, w, res, norm_w.reshape(1, n))


def _merge_kernel(oa_ref, ub_ref, ga_ref, gb_ref, x_ref, wa_ref, wb_ref, wm_ref, nw_ref, x1_ref, xn_ref):
    ya = _dot(oa_ref[...], wa_ref[...])
    yb = _dot(ub_ref[...], wb_ref[...])
    merged = _sigmoid(ga_ref[...].astype(F32)) * ya + _sigmoid(gb_ref[...].astype(F32)) * yb
    x1 = x_ref[...] + _dot(merged, wm_ref[...])
    x1_ref[...] = x1
    xn_ref[...] = _rms(x1, nw_ref[...]).astype(xn_ref.dtype)


def merge_mixers(o_a, u_b, proj, x, w_gdn_o, w_sc_o, w_mix_out, norm_w, tm):
    rows, d = x.shape
    row_blk = lambda i: (i, 0)
    full = lambda i: (0, 0)
    return pl.pallas_call(
        _merge_kernel,
        grid=(rows // tm,),
        in_specs=[
            pl.BlockSpec((tm, d), row_blk),
            pl.BlockSpec((tm, d), row_blk),
            pl.BlockSpec((tm, d), lambda i: (i, COL_GA // D_MODEL)),
            pl.BlockSpec((tm, d), lambda i: (i, COL_GB // D_MODEL)),
            pl.BlockSpec((tm, d), row_blk),
            pl.BlockSpec((d, d), full),
            pl.BlockSpec((d, d), full),
            pl.BlockSpec((d, d), full),
            pl.BlockSpec((1, d), full),
        ],
        out_specs=[pl.BlockSpec((tm, d), row_blk), pl.BlockSpec((tm, d), row_blk)],
        out_shape=[jax.ShapeDtypeStruct((rows, d), F32), jax.ShapeDtypeStruct((rows, d), BF16)],
        compiler_params=_params(1),
        name="merge_mixers",
    )(o_a, u_b, proj, proj, x, w_gdn_o, w_sc_o, w_mix_out, norm_w.reshape(1, d))


def _mm_res_final_kernel(a_ref, w_ref, res_ref, nw_ref, y_ref):
    x = res_ref[...] + _dot(a_ref[...], w_ref[...])
    y_ref[...] = _rms(x, nw_ref[...])


def matmul_residual_final_norm(a, w, res, norm_w, tm):
    rows, k = a.shape
    n = w.shape[1]
    return pl.pallas_call(
        _mm_res_final_kernel,
        grid=(rows // tm,),
        in_specs=[
            pl.BlockSpec((tm, k), lambda i: (i, 0)),
            pl.BlockSpec((k, n), lambda i: (0, 0)),
            pl.BlockSpec((tm, n), lambda i: (i, 0)),
            pl.BlockSpec((1, n), lambda i: (0, 0)),
        ],
        out_specs=pl.BlockSpec((tm, n), lambda i: (i, 0)),
        out_shape=jax.ShapeDtypeStruct((rows, n), F32),
        compiler_params=_params(1),
        name="matmul_residual_final_norm",
    )(a, w, res, norm_w.reshape(1, n))


def _shifted(x, prev_tail, s):
    rolled = pltpu.roll(x, s, 0)
    head = pltpu.roll(prev_tail, s, 0)
    row = lax.broadcasted_iota(jnp.int32, head.shape, 0)
    first = jnp.where(row < s, head, rolled[:SUBLANES])
    if x.shape[0] == SUBLANES:
        return first
    return jnp.concatenate([first, rolled[SUBLANES:]], axis=0)


def _causal_conv(x, prev_tail, w):
    width = w.shape[0]
    y = x * w[width - 1:width]
    for i in range(width - 1):
        y = y + _shifted(x, prev_tail, width - 1 - i) * w[i:i + 1]
    return y


def _l2norm_heads(x, scale):
    outs = []
    for h in range(GDN_HEADS):
        xh = x[:, h * HEAD_DIM:(h + 1) * HEAD_DIM]
        outs.append(xh * (lax.rsqrt(jnp.sum(xh * xh, axis=-1, keepdims=True) + EPS) * scale))
    return jnp.concatenate(outs, axis=1)


def _prep_body(qkv, bg, cg, xin, qkv_tail, cx_tail, cw, sw):
    conv = _silu(_causal_conv(qkv, qkv_tail, cw))
    q = _l2norm_heads(conv[:, :D_MODEL], HEAD_DIM ** -0.5)
    k = _l2norm_heads(conv[:, D_MODEL:2 * D_MODEL], 1.0)
    v = conv[:, 2 * D_MODEL:]
    cx = cg * xin
    u = bg * _causal_conv(cx, cx_tail, sw)
    return q, k, v, u, cx


def _prep_prompt_kernel(qkv_ref, bg_ref, cg_ref, xin_ref, cw_ref, sw_ref,
                        q_ref, k_ref, v_ref, u_ref, cxt_ref, qkv_tail, cx_tail):
    @pl.when(pl.program_id(1) == 0)
    def _():
        qkv_tail[...] = jnp.zeros_like(qkv_tail)
        cx_tail[...] = jnp.zeros_like(cx_tail)

    qkv = qkv_ref[...].astype(F32)
    q, k, v, u, cx = _prep_body(qkv, bg_ref[...].astype(F32), cg_ref[...].astype(F32), xin_ref[...].astype(F32),
                                qkv_tail[...], cx_tail[...], cw_ref[...], sw_ref[...])
    q_ref[...] = q.astype(q_ref.dtype)
    k_ref[...] = k.astype(k_ref.dtype)
    v_ref[...] = v.astype(v_ref.dtype)
    u_ref[...] = u.astype(u_ref.dtype)
    tm = qkv.shape[0]
    qkv_tail[...] = qkv[tm - SUBLANES:]
    cx_tail[...] = cx[tm - SUBLANES:]
    cxt_ref[0] = cx[tm - SUBLANES:]


def prep_prompt(proj, conv_w, sc_w, batch, seq, tm):
    rows = batch * seq
    nt = seq // tm
    d = D_MODEL
    blk = lambda c: pl.BlockSpec((tm, d), lambda b, t, c=c: (b * nt + t, c))
    out_rows = pl.BlockSpec((tm, d), lambda b, t: (b * nt + t, 0))
    return pl.pallas_call(
        _prep_prompt_kernel,
        grid=(batch, nt),
        in_specs=[
            pl.BlockSpec((tm, 3 * d), lambda b, t: (b * nt + t, 0)),
            blk(COL_BG // d), blk(COL_CG // d), blk(COL_XIN // d),
            pl.BlockSpec((GDN_CONV_W, 3 * d), lambda b, t: (0, 0)),
            pl.BlockSpec((SC_CONV_W, d), lambda b, t: (0, 0)),
        ],
        out_specs=[out_rows, out_rows, out_rows, out_rows, pl.BlockSpec((1, SUBLANES, d), lambda b, t: (b, 0, 0))],
        out_shape=[jax.ShapeDtypeStruct((rows, d), BF16)] * 4 + [jax.ShapeDtypeStruct((batch, SUBLANES, d), F32)],
        scratch_shapes=[pltpu.VMEM((SUBLANES, 3 * d), F32), pltpu.VMEM((SUBLANES, d), F32)],
        compiler_params=_params(2),
        name="prep_prompt",
    )(proj, proj, proj, proj, conv_w, sc_w)


def _prep_sample_kernel(qkv_ref, bg_ref, cg_ref, xin_ref, qst_ref, cst_ref, cw_ref, sw_ref,
                        q_ref, k_ref, v_ref, u_ref, cx_ref):
    q, k, v, u, cx = _prep_body(qkv_ref[...], bg_ref[...], cg_ref[...], xin_ref[...],
                                qst_ref[0], cst_ref[0], cw_ref[...], sw_ref[...])
    q_ref[...] = q
    k_ref[...] = k
    v_ref[...] = v
    u_ref[...] = u
    cx_ref[...] = cx


def prep_sample(proj, qkv_state8, cx_state8, conv_w, sc_w, n_seq):
    rows = n_seq * SUBLANES
    d = D_MODEL
    t = SUBLANES
    blk = lambda c: pl.BlockSpec((t, d), lambda n, c=c: (n, c))
    out_rows = pl.BlockSpec((t, d), lambda n: (n, 0))
    return pl.pallas_call(
        _prep_sample_kernel,
        grid=(n_seq,),
        in_specs=[
            pl.BlockSpec((t, 3 * d), lambda n: (n, 0)),
            blk(COL_BG // d), blk(COL_CG // d), blk(COL_XIN // d),
            pl.BlockSpec((1, t, 3 * d), lambda n: (n, 0, 0)),
            pl.BlockSpec((1, t, d), lambda n: (n, 0, 0)),
            pl.BlockSpec((GDN_CONV_W, 3 * d), lambda n: (0, 0)),
            pl.BlockSpec((SC_CONV_W, d), lambda n: (0, 0)),
        ],
        out_specs=[out_rows] * 5,
        out_shape=[jax.ShapeDtypeStruct((rows, d), F32)] * 5,
        compiler_params=_params(1),
        name="prep_sample",
    )(proj, proj, proj, proj, qkv_state8, cx_state8, conv_w, sc_w)


def _tri_inverse(l_mats, n, top):
    ri = lax.broadcasted_iota(jnp.int32, (n, n), 0)
    ci = lax.broadcasted_iota(jnp.int32, (n, n), 1)
    eye = jnp.where(ri == ci, 1.0, 0.0)
    ts = [eye - jnp.where((ri ^ ci) == 1, l, 0.0) for l in l_mats]
    shift = 1
    while (1 << shift) < top:
        pair = ((ri >> shift) ^ (ci >> shift)) == 1
        xs = [_dot(jnp.where(pair, l, 0.0), t) for l, t in zip(l_mats, ts)]
        ts = [t - _dot(t, x) for t, x in zip(ts, xs)]
        shift += 1
    return ts


def _pair_diff(col_b):
    hi, lo = _split_hi_lo(col_b)
    hi = hi.astype(F32)
    lo = lo.astype(F32)
    lane = lax.broadcasted_iota(jnp.int32, col_b.shape, 1)
    left = jnp.where(lane == 0, hi, jnp.where(lane == 1, lo, jnp.where(lane < 4, 1.0, 0.0)))
    right = jnp.where(lane == 2, -hi, jnp.where(lane == 3, -lo, jnp.where(lane < 2, 1.0, 0.0)))
    return _dot_nt(left, right)


def _gdn_intra(qs, ks, vs, gcum_bs, beta_bs, block):
    n = qs[0].shape[0]
    ri = lax.broadcasted_iota(jnp.int32, (n, n), 0)
    ci = lax.broadcasted_iota(jnp.int32, (n, n), 1)
    same = (ri >> (block.bit_length() - 1)) == (ci >> (block.bit_length() - 1))
    diffs = [_pair_diff(g) for g in gcum_bs]
    decays = [jnp.where(same, jnp.where(ci <= ri, jnp.exp(jnp.minimum(d, 0.0)), 0.0), 0.0) for d in diffs]
    kks = [_dot_nt(k, k) for k in ks]
    l_mats = [b[:, :n] * kk * jnp.where(ci < ri, dec, 0.0) for b, kk, dec in zip(beta_bs, kks, decays)]
    ts = _tri_inverse(l_mats, n, block)
    egcs = [jnp.exp(g) for g in gcum_bs]
    wus = [_dot(t, jnp.concatenate([k * (b * e), v * b], axis=1))
           for t, k, v, b, e in zip(ts, ks, vs, beta_bs, egcs)]
    ps = [_dot_nt(q, k) * dec for q, k, dec in zip(qs, ks, decays)]
    return ([wu[:, :HEAD_DIM] for wu in wus], [wu[:, HEAD_DIM:] for wu in wus],
            [q * e for q, e in zip(qs, egcs)], ps)


def _gated_out(o, z, onw):
    return _rms(o, onw) * _silu(z)


def _gdn_prompt_kernel(q_ref, k_ref, v_ref, z_ref, gb_ref, onw_ref, o_ref, s_out_ref, s_ref):
    c = pl.program_id(1)

    @pl.when(c == 0)
    def _():
        s_ref[...] = jnp.zeros_like(s_ref)

    n = GDN_CHUNK
    gb = gb_ref[...]
    ri = lax.broadcasted_iota(jnp.int32, (n, n), 0)
    ci = lax.broadcasted_iota(jnp.int32, (n, n), 1)
    gcum = _dot_exact_lhs(jnp.where(ci <= ri, 1.0, 0.0), gb)
    heads = range(GDN_HEADS)
    sls = [slice(h * HEAD_DIM, (h + 1) * HEAD_DIM) for h in heads]
    qs = [q_ref[:, sl].astype(F32) for sl in sls]
    ks = [k_ref[:, sl].astype(F32) for sl in sls]
    vs = [v_ref[:, sl].astype(F32) for sl in sls]
    gcum_bs = [jnp.broadcast_to(gcum[:, h:h + 1], (n, HEAD_DIM)) for h in heads]
    beta_bs = [jnp.broadcast_to(gb[:, GDN_HEADS + h:GDN_HEADS + h + 1], (n, HEAD_DIM)) for h in heads]
    ws, uvs, qes, ps = _gdn_intra(qs, ks, vs, gcum_bs, beta_bs, n)
    ss = [s_ref[h] for h in heads]
    wqs = [_dot(jnp.concatenate([w, qe], axis=0), s) for w, qe, s in zip(ws, qes, ss)]
    us = [uv - wq[:n] for uv, wq in zip(uvs, wqs)]
    os_ = [wq[n:] + _dot(p, u) for wq, p, u in zip(wqs, ps, us)]
    glasts = [g[n - 1:n, :] for g in gcum_bs]
    kes = [k * jnp.exp(gl - g) for k, gl, g in zip(ks, glasts, gcum_bs)]
    s_news = [jnp.exp(gl) * s + _dot_tn(ke, u) for gl, s, ke, u in zip(glasts, ss, kes, us)]
    for h in heads:
        s_ref[h] = s_news[h]
        o_ref[:, sls[h]] = _gated_out(os_[h], z_ref[:, sls[h]].astype(F32), onw_ref[...]).astype(o_ref.dtype)

    @pl.when(c == pl.num_programs(1) - 1)
    def _():
        s_out_ref[0] = s_ref[...]


def gdn_prompt(q, k, v, proj, gb, out_norm_w, batch, seq):
    rows = batch * seq
    n = GDN_CHUNK
    nc = seq // n
    d = D_MODEL
    rowblk = pl.BlockSpec((n, d), lambda b, c: (b * nc + c, 0))
    return pl.pallas_call(
        _gdn_prompt_kernel,
        grid=(batch, nc),
        in_specs=[
            rowblk, rowblk, rowblk,
            pl.BlockSpec((n, d), lambda b, c: (b * nc + c, COL_Z // d)),
            pl.BlockSpec((n, 2 * GDN_HEADS), lambda b, c: (b * nc + c, 0)),
            pl.BlockSpec((1, HEAD_DIM), lambda b, c: (0, 0)),
        ],
        out_specs=[rowblk, pl.BlockSpec((1, GDN_HEADS, HEAD_DIM, HEAD_DIM), lambda b, c: (b, 0, 0, 0))],
        out_shape=[jax.ShapeDtypeStruct((rows, d), BF16),
                   jax.ShapeDtypeStruct((batch, GDN_HEADS, HEAD_DIM, HEAD_DIM), F32)],
        scratch_shapes=[pltpu.VMEM((GDN_HEADS, HEAD_DIM, HEAD_DIM), F32)],
        compiler_params=_params(2),
        name="gdn_prompt",
    )(q, k, v, proj, gb, out_norm_w.reshape(1, HEAD_DIM))


def _stack_heads(x):
    return jnp.concatenate([x[:, h * HEAD_DIM:(h + 1) * HEAD_DIM] for h in range(GDN_HEADS)], axis=0)


def _gdn_sample_kernel(q_ref, k_ref, v_ref, z_ref, gb_ref, onw_ref, s_in_ref, o_ref, s_out_ref, *, n_seq):
    t = SUBLANES
    n = GDN_HEADS * t
    seqs = range(n_seq)
    heads = range(GDN_HEADS)
    rows = [slice(i * t, (i + 1) * t) for i in seqs]
    qs = [_stack_heads(q_ref[r, :]) for r in rows]
    ks = [_stack_heads(k_ref[r, :]) for r in rows]
    vs = [_stack_heads(v_ref[r, :]) for r in rows]
    gbs = [gb_ref[r, :] for r in rows]
    g_bs = [jnp.concatenate([jnp.broadcast_to(gb[:, h:h + 1], (t, HEAD_DIM)) for h in heads], axis=0) for gb in gbs]
    beta_bs = [jnp.concatenate([jnp.broadcast_to(gb[:, GDN_HEADS + h:GDN_HEADS + h + 1], (t, HEAD_DIM))
                                for h in heads], axis=0) for gb in gbs]
    ri = lax.broadcasted_iota(jnp.int32, (n, n), 0)
    ci = lax.broadcasted_iota(jnp.int32, (n, n), 1)
    cum_mask = jnp.where(((ri >> 3) == (ci >> 3)), jnp.where(ci <= ri, 1.0, 0.0), 0.0)
    gcum_bs = [_dot_exact_lhs(cum_mask, g) for g in g_bs]
    ws, uvs, qes, ps = _gdn_intra(qs, ks, vs, gcum_bs, beta_bs, t)
    glast_bs = [jnp.concatenate([jnp.broadcast_to(g[h * t + t - 1:h * t + t, :], (t, HEAD_DIM)) for h in heads],
                                axis=0) for g in gcum_bs]
    kes = [k * jnp.exp(gl - g) for k, gl, g in zip(ks, glast_bs, gcum_bs)]
    wqes = [jnp.concatenate([w, qe], axis=0) for w, qe in zip(ws, qes)]
    wqs = [[_dot(wqes[i], s_in_ref[i, h]) for h in heads] for i in seqs]
    us = [uvs[i] - jnp.concatenate([wqs[i][h][h * t:(h + 1) * t] for h in heads], axis=0) for i in seqs]
    os_ = [jnp.concatenate([wqs[i][h][n + h * t:n + (h + 1) * t] for h in heads], axis=0) + _dot(ps[i], us[i])
           for i in seqs]
    row = lax.broadcasted_iota(jnp.int32, (n, HEAD_DIM), 0)
    for i in seqs:
        for h in heads:
            u_h = jnp.where((row >> 3) == h, us[i], 0.0)
            s_out_ref[i, h] = jnp.exp(glast_bs[i][h * t:h * t + 1, :]) * s_in_ref[i, h] + _dot_tn(kes[i], u_h)
    for i in seqs:
        og = _gated_out(os_[i], _stack_heads(z_ref[rows[i], :]), onw_ref[...])
        for h in heads:
            o_ref[rows[i], h * HEAD_DIM:(h + 1) * HEAD_DIM] = og[h * t:(h + 1) * t]


def gdn_sample(q, k, v, proj, gb, out_norm_w, state, n_seq, per_step):
    t = SUBLANES
    d = D_MODEL
    rowblk = pl.BlockSpec((per_step * t, d), lambda i: (i, 0))
    sblk = pl.BlockSpec((per_step, GDN_HEADS, HEAD_DIM, HEAD_DIM), lambda i: (i, 0, 0, 0))
    return pl.pallas_call(
        functools.partial(_gdn_sample_kernel, n_seq=per_step),
        grid=(n_seq // per_step,),
        in_specs=[
            rowblk, rowblk, rowblk,
            pl.BlockSpec((per_step * t, d), lambda i: (i, COL_Z // d)),
            pl.BlockSpec((per_step * t, 2 * GDN_HEADS), lambda i: (i, 0)),
            pl.BlockSpec((1, HEAD_DIM), lambda i: (0, 0)),
            sblk,
        ],
        out_specs=[rowblk, sblk],
        out_shape=[jax.ShapeDtypeStruct((n_seq * t, d), F32),
                   jax.ShapeDtypeStruct((n_seq, GDN_HEADS, HEAD_DIM, HEAD_DIM), F32)],
        compiler_params=_params(1),
        name="gdn_sample",
    )(q, k, v, proj, gb, out_norm_w.reshape(1, HEAD_DIM), state)


def _attend(q, mk_ref, mv_ref, i):
    outs = []
    for h in range(X_HEADS):
        sl = slice(h * X_HEAD_DIM, (h + 1) * X_HEAD_DIM)
        s = _dot_nt(q[:, sl], mk_ref[i, :, sl]) * (X_HEAD_DIM ** -0.5)
        e = jnp.exp(s - jnp.max(s, axis=-1, keepdims=True))
        p = e / jnp.sum(e, axis=-1, keepdims=True)
        outs.append(_dot(p, mv_ref[i, :, sl]))
    return jnp.concatenate(outs, axis=1)


def _attn_prompt_kernel(q_ref, mk_ref, mv_ref, o_ref):
    o_ref[...] = _attend(q_ref[...], mk_ref, mv_ref, 0).astype(o_ref.dtype)


def attention_prompt(q, mkv, batch, seq, tm):
    nt = seq // tm
    d = D_MODEL
    return pl.pallas_call(
        _attn_prompt_kernel,
        grid=(batch, nt),
        in_specs=[
            pl.BlockSpec((tm, d), lambda b, t: (b * nt + t, 0)),
            pl.BlockSpec((1, N_MEM, d), lambda b, t: (b, 0, 0)),
            pl.BlockSpec((1, N_MEM, d), lambda b, t: (b, 0, 1)),
        ],
        out_specs=pl.BlockSpec((tm, d), lambda b, t: (b * nt + t, 0)),
        out_shape=jax.ShapeDtypeStruct((batch * seq, d), BF16),
        compiler_params=_params(2),
        name="attention_prompt",
    )(q, mkv, mkv)


def _attn_sample_kernel(q_ref, mk_ref, mv_ref, o_ref, *, n_seq):
    t = SUBLANES
    for i in range(n_seq):
        q = q_ref[i * t:(i + 1) * t, :]
        q16 = jnp.concatenate([q, jnp.zeros_like(q)], axis=0)
        o_ref[i * t:(i + 1) * t, :] = _attend(q16, mk_ref, mv_ref, i)[:t]


def attention_sample(q, mem_k, mem_v, n_seq, per_step):
    t = SUBLANES
    d = D_MODEL
    return pl.pallas_call(
        functools.partial(_attn_sample_kernel, n_seq=per_step),
        grid=(n_seq // per_step,),
        in_specs=[
            pl.BlockSpec((per_step * t, d), lambda i: (i, 0)),
            pl.BlockSpec((per_step, N_MEM, d), lambda i: (i, 0, 0)),
            pl.BlockSpec((per_step, N_MEM, d), lambda i: (i, 0, 0)),
        ],
        out_specs=pl.BlockSpec((per_step * t, d), lambda i: (i, 0)),
        out_shape=jax.ShapeDtypeStruct((n_seq * t, d), F32),
        compiler_params=_params(1),
        name="attention_sample",
    )(q, mem_k, mem_v)


def _layer_tail(x1, xn1, ctx_fn, w, tm, tn):
    hq = matmul_rows(xn1, w["xq"], xn1.dtype, tm, tn)
    ctx = ctx_fn(hq)
    x2, xn2 = matmul_residual_norm(ctx, w["xo"], x1, w["norm_mlp"], BF16, tm)
    hm = matmul_rows(xn2, w["up"], BF16, tm, tn, relu2=True)
    return matmul_residual_final_norm(hm, w["down"], x2, w["norm_f"], tm)


def kernel(x_prompt, x_sample, mem_prompt, cache_mem_k, cache_mem_v, state_gdn_conv, state_gdn, state_sc_conv, norm_mix_w, w_in, gdn_conv_w, gdn_A_log, gdn_dt_bias, gdn_out_norm_w, w_gdn_o, sc_conv_w, w_sc_o, w_mix_out, norm_x_w, mem_norm_w, w_xq, w_xkv, w_xo, norm_mlp_w, w_mlp_up, w_mlp_down, norm_f_w):
    batch, seq, d = x_prompt.shape
    n_seq, t_dec, _ = x_sample.shape
    assert norm_mix_w.shape[0] == 1 and t_dec == SUBLANES and d == D_MODEL
    ab0 = 4 * D_MODEL
    w_in0 = w_in[0]
    w = {
        "main": jnp.concatenate([w_in0[:, :ab0], w_in0[:, ab0 + 2 * GDN_HEADS:]], axis=1).astype(BF16),
        "ab": w_in0[:, ab0:ab0 + 2 * GDN_HEADS],
        "gdn_o": w_gdn_o[0].astype(BF16), "sc_o": w_sc_o[0].astype(BF16), "mix": w_mix_out[0].astype(BF16),
        "xq": w_xq[0].astype(BF16), "xkv": w_xkv[0].astype(BF16), "xo": w_xo[0].astype(BF16),
        "up": w_mlp_up[0].astype(BF16), "down": w_mlp_down[0].astype(BF16),
        "norm_mlp": norm_mlp_w[0], "norm_f": norm_f_w,
    }
    tm, tn = 512, 1024

    rows_p = batch * seq
    xp = x_prompt.reshape(rows_p, d)
    memn = rmsnorm_rows(mem_prompt.reshape(batch * N_MEM, d), mem_norm_w[0], tm)
    mkv = matmul_rows(memn, w["xkv"], F32, tm, tn)
    xn, gb = rmsnorm_gates_rows(xp, norm_mix_w[0], w["ab"], gdn_A_log[0], gdn_dt_bias[0], tm)
    proj = matmul_rows(xn, w["main"], BF16, tm, tn)
    q, k, v, u_sc, cx_tail = prep_prompt(proj, gdn_conv_w[0], sc_conv_w[0], batch, seq, 256)
    o_gdn, s_prompt = gdn_prompt(q, k, v, proj, gb, gdn_out_norm_w[0], batch, seq)
    x1, xn1 = merge_mixers(o_gdn, u_sc, proj, xp, w["gdn_o"], w["sc_o"], w["mix"], norm_x_w[0], tm)
    mkv3 = mkv.reshape(batch, N_MEM, 2 * d)
    y_prompt = _layer_tail(x1, xn1, lambda hq: attention_prompt(hq, mkv3, batch, seq, tm), w, tm, tn)
    p_conv = proj.reshape(batch, seq, MAIN_COLS)[:, seq - (GDN_CONV_W - 1):, :3 * d].astype(F32)
    p_sc = cx_tail[:, SUBLANES - (SC_CONV_W - 1):]
    p_mk = mkv3[:, :, :d].reshape(batch, N_MEM, X_HEADS, X_HEAD_DIM)
    p_mv = mkv3[:, :, d:].reshape(batch, N_MEM, X_HEADS, X_HEAD_DIM)

    rows_s = n_seq * t_dec
    xs = x_sample.reshape(rows_s, d)
    xn_s, gb_s = rmsnorm_gates_rows(xs, norm_mix_w[0], w["ab"], gdn_A_log[0], gdn_dt_bias[0], tm)
    proj_s = matmul_rows(xn_s, w["main"], F32, tm, tn)
    qst8 = jnp.pad(state_gdn_conv[0], ((0, 0), (SUBLANES - (GDN_CONV_W - 1), 0), (0, 0)))
    cst8 = jnp.pad(state_sc_conv[0], ((0, 0), (SUBLANES - (SC_CONV_W - 1), 0), (0, 0)))
    q_s, k_s, v_s, u_s, cx_s = prep_sample(proj_s, qst8, cst8, gdn_conv_w[0], sc_conv_w[0], n_seq)
    o_s, s_sample = gdn_sample(q_s, k_s, v_s, proj_s, gb_s, gdn_out_norm_w[0], state_gdn[0], n_seq, 4)
    x1_s, xn1_s = merge_mixers(o_s, u_s, proj_s, xs, w["gdn_o"], w["sc_o"], w["mix"], norm_x_w[0], tm)
    mk_s = cache_mem_k[0].reshape(n_seq, N_MEM, d)
    mv_s = cache_mem_v[0].reshape(n_seq, N_MEM, d)
    y_sample = _layer_tail(x1_s, xn1_s, lambda hq: attention_sample(hq.astype(F32), mk_s, mv_s, n_seq, 4), w, tm, tn)
    s_conv = proj_s.reshape(n_seq, t_dec, MAIN_COLS)[:, t_dec - (GDN_CONV_W - 1):, :3 * d]
    s_sc = cx_s.reshape(n_seq, t_dec, d)[:, t_dec - (SC_CONV_W - 1):]

    return (y_prompt.reshape(batch, seq, d), y_sample.reshape(n_seq, t_dec, d),
            p_mk[None], p_mv[None], p_conv[None], s_prompt[None], p_sc[None],
            s_conv[None], s_sample[None], s_sc[None])
```

```python
import functools

import jax
import jax.numpy as jnp
from jax import lax
from jax.experimental import pallas as pl
from jax.experimental.pallas import tpu as pltpu

F32 = jnp.float32
BF16 = jnp.bfloat16

D_MODEL = 1024
GDN_HEADS = 8
HEAD_DIM = 128
GDN_CHUNK = 64
GDN_CONV_W = 4
SC_CONV_W = 3
X_HEADS = 4
X_HEAD_DIM = 256
N_MEM = 256
D_FF = 4096
EPS = 1e-6
SUBLANES = 8
VMEM_LIMIT = 48 * 1024 * 1024

COL_Z = 3 * D_MODEL
COL_BG = 4 * D_MODEL
COL_CG = 5 * D_MODEL
COL_XIN = 6 * D_MODEL
COL_GA = 7 * D_MODEL
COL_GB = 8 * D_MODEL
MAIN_COLS = 9 * D_MODEL


def _params(n_axes):
    return pltpu.CompilerParams(dimension_semantics=("arbitrary",) * n_axes, vmem_limit_bytes=VMEM_LIMIT)


def _dot(a, b):
    return jnp.dot(a.astype(BF16), b.astype(BF16), preferred_element_type=F32)


def _dot_nt(a, b):
    return lax.dot_general(a.astype(BF16), b.astype(BF16), (((1,), (1,)), ((), ())), preferred_element_type=F32)


def _dot_tn(a, b):
    return lax.dot_general(a.astype(BF16), b.astype(BF16), (((0,), (0,)), ((), ())), preferred_element_type=F32)


def _split_hi_lo(x):
    hi = x.astype(BF16)
    lo = (x - hi.astype(F32)).astype(BF16)
    return hi, lo


def _dot_exact_lhs(m01, x):
    hi, lo = _split_hi_lo(x)
    m = m01.astype(BF16)
    return jnp.dot(m, hi, preferred_element_type=F32) + jnp.dot(m, lo, preferred_element_type=F32)


def _rms(x, w):
    return x * lax.rsqrt(jnp.mean(x * x, axis=-1, keepdims=True) + EPS) * w


def _sigmoid(x):
    return 1.0 / (1.0 + jnp.exp(-x))


def _silu(x):
    return x * _sigmoid(x)


def _softplus(x):
    return jnp.maximum(x, 0.0) + jnp.log(1.0 + jnp.exp(-jnp.abs(x)))


def _norm_kernel(x_ref, w_ref, o_ref):
    o_ref[...] = _rms(x_ref[...], w_ref[...]).astype(o_ref.dtype)


def rmsnorm_rows(x, w, tm):
    rows, d = x.shape
    return pl.pallas_call(
        _norm_kernel,
        grid=(rows // tm,),
        in_specs=[pl.BlockSpec((tm, d), lambda i: (i, 0)), pl.BlockSpec((1, d), lambda i: (0, 0))],
        out_specs=pl.BlockSpec((tm, d), lambda i: (i, 0)),
        out_shape=jax.ShapeDtypeStruct((rows, d), BF16),
        compiler_params=_params(1),
        name="rmsnorm_rows",
    )(x, w.reshape(1, d))


def _norm_gates_kernel(x_ref, w_ref, wab_ref, alog_ref, dtb_ref, o_ref, gb_ref):
    xn = _rms(x_ref[...], w_ref[...]).astype(BF16)
    o_ref[...] = xn
    ab = jnp.dot(xn, wab_ref[...], preferred_element_type=F32)
    a = ab[:, :GDN_HEADS]
    b = ab[:, GDN_HEADS:2 * GDN_HEADS]
    g = -jnp.exp(alog_ref[...]) * _softplus(a + dtb_ref[...])
    gb_ref[...] = jnp.concatenate([g, _sigmoid(b)], axis=1)


def rmsnorm_gates_rows(x, w, w_ab, a_log, dt_bias, tm):
    rows, d = x.shape
    wab = jnp.zeros((d, 128), BF16).at[:, :2 * GDN_HEADS].set(w_ab.astype(BF16))
    return pl.pallas_call(
        _norm_gates_kernel,
        grid=(rows // tm,),
        in_specs=[
            pl.BlockSpec((tm, d), lambda i: (i, 0)),
            pl.BlockSpec((1, d), lambda i: (0, 0)),
            pl.BlockSpec((d, 128), lambda i: (0, 0)),
            pl.BlockSpec((1, GDN_HEADS), lambda i: (0, 0)),
            pl.BlockSpec((1, GDN_HEADS), lambda i: (0, 0)),
        ],
        out_specs=[pl.BlockSpec((tm, d), lambda i: (i, 0)), pl.BlockSpec((tm, 2 * GDN_HEADS), lambda i: (i, 0))],
        out_shape=[jax.ShapeDtypeStruct((rows, d), BF16), jax.ShapeDtypeStruct((rows, 2 * GDN_HEADS), F32)],
        compiler_params=_params(1),
        name="rmsnorm_gates_rows",
    )(x, w.reshape(1, d), wab, a_log.reshape(1, GDN_HEADS), dt_bias.reshape(1, GDN_HEADS))


def _mm_kernel(a_ref, w_ref, o_ref, *, relu2):
    acc = _dot(a_ref[...], w_ref[...])
    if relu2:
        acc = jnp.square(jnp.maximum(acc, 0.0))
    o_ref[...] = acc.astype(o_ref.dtype)


def matmul_rows(a, w, out_dtype, tm, tn, relu2=False):
    rows, k = a.shape
    n = w.shape[1]
    return pl.pallas_call(
        functools.partial(_mm_kernel, relu2=relu2),
        grid=(n // tn, rows // tm),
        in_specs=[pl.BlockSpec((tm, k), lambda j, i: (i, 0)), pl.BlockSpec((k, tn), lambda j, i: (0, j))],
        out_specs=pl.BlockSpec((tm, tn), lambda j, i: (i, j)),
        out_shape=jax.ShapeDtypeStruct((rows, n), out_dtype),
        compiler_params=_params(2),
        name="matmul_rows",
    )(a, w)


def _mm_res_norm_kernel(a_ref, w_ref, res_ref, nw_ref, x_ref, xn_ref):
    x = res_ref[...] + _dot(a_ref[...], w_ref[...])
    x_ref[...] = x
    xn_ref[...] = _rms(x, nw_ref[...]).astype(xn_ref.dtype)


def matmul_residual_norm(a, w, res, norm_w, xn_dtype, tm):
    rows, k = a.shape
    n = w.shape[1]
    return pl.pallas_call(
        _mm_res_norm_kernel,
        grid=(rows // tm,),
        in_specs=[
            pl.BlockSpec((tm, k), lambda i: (i, 0)),
            pl.BlockSpec((k, n), lambda i: (0, 0)),
            pl.BlockSpec((tm, n), lambda i: (i, 0)),
            pl.BlockSpec((1, n), lambda i: (0, 0)),
        ],
        out_specs=[pl.BlockSpec((tm, n), lambda i: (i, 0)), pl.BlockSpec((tm, n), lambda i: (i, 0))],
        out_shape=[jax.ShapeDtypeStruct((rows, n), F32), jax.ShapeDtypeStruct((rows, n), xn_dtype)],
        compiler_params=_params(1),
        name="matmul_residual_norm",
    )(a, w, res, norm_w.reshape(1, n))


def _merge_kernel(oa_ref, ub_ref, ga_ref, gb_ref, x_ref, wa_ref, wb_ref, wm_ref, nw_ref, x1_ref, xn_ref):
    ya = _dot(oa_ref[...], wa_ref[...])
    yb = _dot(ub_ref[...], wb_ref[...])
    merged = _sigmoid(ga_ref[...].astype(F32)) * ya + _sigmoid(gb_ref[...].astype(F32)) * yb
    x1 = x_ref[...] + _dot(merged, wm_ref[...])
    x1_ref[...] = x1
    xn_ref[...] = _rms(x1, nw_ref[...]).astype(xn_ref.dtype)


def merge_mixers(o_a, u_b, proj, x, w_gdn_o, w_sc_o, w_mix_out, norm_w, tm):
    rows, d = x.shape
    row_blk = lambda i: (i, 0)
    full = lambda i: (0, 0)
    return pl.pallas_call(
        _merge_kernel,
        grid=(rows // tm,),
        in_specs=[
            pl.BlockSpec((tm, d), row_blk),
            pl.BlockSpec((tm, d), row_blk),
            pl.BlockSpec((tm, d), lambda i: (i, COL_GA // D_MODEL)),
            pl.BlockSpec((tm, d), lambda i: (i, COL_GB // D_MODEL)),
            pl.BlockSpec((tm, d), row_blk),
            pl.BlockSpec((d, d), full),
            pl.BlockSpec((d, d), full),
            pl.BlockSpec((d, d), full),
            pl.BlockSpec((1, d), full),
        ],
        out_specs=[pl.BlockSpec((tm, d), row_blk), pl.BlockSpec((tm, d), row_blk)],
        out_shape=[jax.ShapeDtypeStruct((rows, d), F32), jax.ShapeDtypeStruct((rows, d), BF16)],
        compiler_params=_params(1),
        name="merge_mixers",
    )(o_a, u_b, proj, proj, x, w_gdn_o, w_sc_o, w_mix_out, norm_w.reshape(1, d))


def _mm_res_final_kernel(a_ref, w_ref, res_ref, nw_ref, y_ref):
    x = res_ref[...] + _dot(a_ref[...], w_ref[...])
    y_ref[...] = _rms(x, nw_ref[...])


def matmul_residual_final_norm(a, w, res, norm_w, tm):
    rows, k = a.shape
    n = w.shape[1]
    return pl.pallas_call(
        _mm_res_final_kernel,
        grid=(rows // tm,),
        in_specs=[
            pl.BlockSpec((tm, k), lambda i: (i, 0)),
            pl.BlockSpec((k, n), lambda i: (0, 0)),
            pl.BlockSpec((tm, n), lambda i: (i, 0)),
            pl.BlockSpec((1, n), lambda i: (0, 0)),
        ],
        out_specs=pl.BlockSpec((tm, n), lambda i: (i, 0)),
        out_shape=jax.ShapeDtypeStruct((rows, n), F32),
        compiler_params=_params(1),
        name="matmul_residual_final_norm",
    )(a, w, res, norm_w.reshape(1, n))


def _shifted(x, prev_tail, s):
    rolled = pltpu.roll(x, s, 0)
    head = pltpu.roll(prev_tail, s, 0)
    row = lax.broadcasted_iota(jnp.int32, head.shape, 0)
    first = jnp.where(row < s, head, rolled[:SUBLANES])
    if x.shape[0] == SUBLANES:
        return first
    return jnp.concatenate([first, rolled[SUBLANES:]], axis=0)


def _shifted_groups(x, hist, s, n_hist):
    rows = x.shape[0]
    rolled = pltpu.roll(x, s, 0)
    head = hist if n_hist == s else pltpu.roll(hist, rows - (n_hist - s), 0)
    row = lax.broadcasted_iota(jnp.int32, x.shape, 0)
    return jnp.where((row & (SUBLANES - 1)) < s, head, rolled)


def _causal_conv(x, w, shift_fn):
    width = w.shape[0]
    y = x * w[width - 1:width]
    for i in range(width - 1):
        y = y + shift_fn(width - 1 - i) * w[i:i + 1]
    return y


def _l2norm_heads(x, scale):
    outs = []
    for h in range(GDN_HEADS):
        xh = x[:, h * HEAD_DIM:(h + 1) * HEAD_DIM]
        outs.append(xh * (lax.rsqrt(jnp.sum(xh * xh, axis=-1, keepdims=True) + EPS) * scale))
    return jnp.concatenate(outs, axis=1)


def _prep_body(qkv, bg, cg, xin, cw, sw, qkv_shift, cx_shift):
    conv = _silu(_causal_conv(qkv, cw, qkv_shift))
    q = _l2norm_heads(conv[:, :D_MODEL], HEAD_DIM ** -0.5)
    k = _l2norm_heads(conv[:, D_MODEL:2 * D_MODEL], 1.0)
    v = conv[:, 2 * D_MODEL:]
    cx = cg * xin
    u = bg * _causal_conv(cx, sw, functools.partial(cx_shift, cx))
    return q, k, v, u, cx


def _prep_prompt_kernel(qkv_ref, bg_ref, cg_ref, xin_ref, cw_ref, sw_ref,
                        q_ref, k_ref, v_ref, u_ref, cxt_ref, qkv_tail, cx_tail):
    @pl.when(pl.program_id(1) == 0)
    def _():
        qkv_tail[...] = jnp.zeros_like(qkv_tail)
        cx_tail[...] = jnp.zeros_like(cx_tail)

    qkv = qkv_ref[...].astype(F32)
    q, k, v, u, cx = _prep_body(qkv, bg_ref[...].astype(F32), cg_ref[...].astype(F32), xin_ref[...].astype(F32),
                                cw_ref[...], sw_ref[...],
                                lambda s: _shifted(qkv, qkv_tail[...], s),
                                lambda cx, s: _shifted(cx, cx_tail[...], s))
    q_ref[...] = q.astype(q_ref.dtype)
    k_ref[...] = k.astype(k_ref.dtype)
    v_ref[...] = v.astype(v_ref.dtype)
    u_ref[...] = u.astype(u_ref.dtype)
    tm = qkv.shape[0]
    qkv_tail[...] = qkv[tm - SUBLANES:]
    cx_tail[...] = cx[tm - SUBLANES:]
    cxt_ref[0] = cx[tm - SUBLANES:]


def prep_prompt(proj, conv_w, sc_w, batch, seq, tm):
    rows = batch * seq
    nt = seq // tm
    d = D_MODEL
    blk = lambda c: pl.BlockSpec((tm, d), lambda b, t, c=c: (b * nt + t, c))
    out_rows = pl.BlockSpec((tm, d), lambda b, t: (b * nt + t, 0))
    return pl.pallas_call(
        _prep_prompt_kernel,
        grid=(batch, nt),
        in_specs=[
            pl.BlockSpec((tm, 3 * d), lambda b, t: (b * nt + t, 0)),
            blk(COL_BG // d), blk(COL_CG // d), blk(COL_XIN // d),
            pl.BlockSpec((GDN_CONV_W, 3 * d), lambda b, t: (0, 0)),
            pl.BlockSpec((SC_CONV_W, d), lambda b, t: (0, 0)),
        ],
        out_specs=[out_rows, out_rows, out_rows, out_rows, pl.BlockSpec((1, SUBLANES, d), lambda b, t: (b, 0, 0))],
        out_shape=[jax.ShapeDtypeStruct((rows, d), BF16)] * 4 + [jax.ShapeDtypeStruct((batch, SUBLANES, d), F32)],
        scratch_shapes=[pltpu.VMEM((SUBLANES, 3 * d), F32), pltpu.VMEM((SUBLANES, d), F32)],
        compiler_params=_params(2),
        name="prep_prompt",
    )(proj, proj, proj, proj, conv_w, sc_w)


def _prep_sample_kernel(qkv_ref, bg_ref, cg_ref, xin_ref, qst_ref, cst_ref, cw_ref, sw_ref,
                        q_ref, k_ref, v_ref, u_ref, cx_ref):
    qkv = qkv_ref[...]
    q, k, v, u, cx = _prep_body(qkv, bg_ref[...], cg_ref[...], xin_ref[...], cw_ref[...], sw_ref[...],
                                lambda s: _shifted_groups(qkv, qst_ref[...], s, GDN_CONV_W - 1),
                                lambda cx, s: _shifted_groups(cx, cst_ref[...], s, SC_CONV_W - 1))
    q_ref[...] = q
    k_ref[...] = k
    v_ref[...] = v
    u_ref[...] = u
    cx_ref[...] = cx


def prep_sample(proj, qkv_state8, cx_state8, conv_w, sc_w, n_seq, per_step):
    rows = n_seq * SUBLANES
    d = D_MODEL
    t = SUBLANES * per_step
    blk = lambda c: pl.BlockSpec((t, d), lambda n, c=c: (n, c))
    out_rows = pl.BlockSpec((t, d), lambda n: (n, 0))
    return pl.pallas_call(
        _prep_sample_kernel,
        grid=(n_seq // per_step,),
        in_specs=[
            pl.BlockSpec((t, 3 * d), lambda n: (n, 0)),
            blk(COL_BG // d), blk(COL_CG // d), blk(COL_XIN // d),
            pl.BlockSpec((t, 3 * d), lambda n: (n, 0)),
            pl.BlockSpec((t, d), lambda n: (n, 0)),
            pl.BlockSpec((GDN_CONV_W, 3 * d), lambda n: (0, 0)),
            pl.BlockSpec((SC_CONV_W, d), lambda n: (0, 0)),
        ],
        out_specs=[out_rows] * 5,
        out_shape=[jax.ShapeDtypeStruct((rows, d), F32)] * 5,
        compiler_params=_params(1),
        name="prep_sample",
    )(proj, proj, proj, proj, qkv_state8, cx_state8, conv_w, sc_w)


def _tri_inverse(l_mats, n, top):
    ri = lax.broadcasted_iota(jnp.int32, (n, n), 0)
    ci = lax.broadcasted_iota(jnp.int32, (n, n), 1)
    eye = jnp.where(ri == ci, 1.0, 0.0)
    ts = [eye - jnp.where((ri ^ ci) == 1, l, 0.0) for l in l_mats]
    shift = 1
    while (1 << shift) < top:
        pair = ((ri >> shift) ^ (ci >> shift)) == 1
        xs = [_dot(jnp.where(pair, l, 0.0), t) for l, t in zip(l_mats, ts)]
        ts = [t - _dot(t, x) for t, x in zip(ts, xs)]
        shift += 1
    return ts


def _pair_diff(col_b):
    hi, lo = _split_hi_lo(col_b)
    hi = hi.astype(F32)
    lo = lo.astype(F32)
    lane = lax.broadcasted_iota(jnp.int32, col_b.shape, 1)
    left = jnp.where(lane == 0, hi, jnp.where(lane == 1, lo, jnp.where(lane < 4, 1.0, 0.0)))
    right = jnp.where(lane == 2, -hi, jnp.where(lane == 3, -lo, jnp.where(lane < 2, 1.0, 0.0)))
    return _dot_nt(left, right)


def _gdn_intra(qs, ks, vs, gcum_bs, beta_bs, block):
    n = qs[0].shape[0]
    ri = lax.broadcasted_iota(jnp.int32, (n, n), 0)
    ci = lax.broadcasted_iota(jnp.int32, (n, n), 1)
    same = (ri >> (block.bit_length() - 1)) == (ci >> (block.bit_length() - 1))
    diffs = [_pair_diff(g) for g in gcum_bs]
    decays = [jnp.where(same, jnp.where(ci <= ri, jnp.exp(jnp.minimum(d, 0.0)), 0.0), 0.0) for d in diffs]
    kks = [_dot_nt(k, k) for k in ks]
    l_mats = [b[:, :n] * kk * jnp.where(ci < ri, dec, 0.0) for b, kk, dec in zip(beta_bs, kks, decays)]
    ts = _tri_inverse(l_mats, n, block)
    egcs = [jnp.exp(g) for g in gcum_bs]
    wus = [_dot(t, jnp.concatenate([k * (b * e), v * b], axis=1))
           for t, k, v, b, e in zip(ts, ks, vs, beta_bs, egcs)]
    ps = [_dot_nt(q, k) * dec for q, k, dec in zip(qs, ks, decays)]
    return ([wu[:, :HEAD_DIM] for wu in wus], [wu[:, HEAD_DIM:] for wu in wus],
            [q * e for q, e in zip(qs, egcs)], ps)


def _gated_out(o, z, onw):
    return _rms(o, onw) * _silu(z)


def _gdn_prompt_kernel(q_ref, k_ref, v_ref, z_ref, gb_ref, onw_ref, o_ref, s_out_ref, s_ref):
    c = pl.program_id(1)

    @pl.when(c == 0)
    def _():
        s_ref[...] = jnp.zeros_like(s_ref)

    n = GDN_CHUNK
    gb = gb_ref[...]
    ri = lax.broadcasted_iota(jnp.int32, (n, n), 0)
    ci = lax.broadcasted_iota(jnp.int32, (n, n), 1)
    gcum = _dot_exact_lhs(jnp.where(ci <= ri, 1.0, 0.0), gb)
    heads = range(GDN_HEADS)
    sls = [slice(h * HEAD_DIM, (h + 1) * HEAD_DIM) for h in heads]
    qs = [q_ref[:, sl].astype(F32) for sl in sls]
    ks = [k_ref[:, sl].astype(F32) for sl in sls]
    vs = [v_ref[:, sl].astype(F32) for sl in sls]
    gcum_bs = [jnp.broadcast_to(gcum[:, h:h + 1], (n, HEAD_DIM)) for h in heads]
    beta_bs = [jnp.broadcast_to(gb[:, GDN_HEADS + h:GDN_HEADS + h + 1], (n, HEAD_DIM)) for h in heads]
    ws, uvs, qes, ps = _gdn_intra(qs, ks, vs, gcum_bs, beta_bs, n)
    ss = [s_ref[h] for h in heads]
    wqs = [_dot(jnp.concatenate([w, qe], axis=0), s) for w, qe, s in zip(ws, qes, ss)]
    us = [uv - wq[:n] for uv, wq in zip(uvs, wqs)]
    os_ = [wq[n:] + _dot(p, u) for wq, p, u in zip(wqs, ps, us)]
    glasts = [g[n - 1:n, :] for g in gcum_bs]
    kes = [k * jnp.exp(gl - g) for k, gl, g in zip(ks, glasts, gcum_bs)]
    s_news = [jnp.exp(gl) * s + _dot_tn(ke, u) for gl, s, ke, u in zip(glasts, ss, kes, us)]
    for h in heads:
        s_ref[h] = s_news[h]
        o_ref[:, sls[h]] = _gated_out(os_[h], z_ref[:, sls[h]].astype(F32), onw_ref[...]).astype(o_ref.dtype)

    @pl.when(c == pl.num_programs(1) - 1)
    def _():
        s_out_ref[0] = s_ref[...]


def gdn_prompt(q, k, v, proj, gb, out_norm_w, batch, seq):
    rows = batch * seq
    n = GDN_CHUNK
    nc = seq // n
    d = D_MODEL
    rowblk = pl.BlockSpec((n, d), lambda b, c: (b * nc + c, 0))
    return pl.pallas_call(
        _gdn_prompt_kernel,
        grid=(batch, nc),
        in_specs=[
            rowblk, rowblk, rowblk,
            pl.BlockSpec((n, d), lambda b, c: (b * nc + c, COL_Z // d)),
            pl.BlockSpec((n, 2 * GDN_HEADS), lambda b, c: (b * nc + c, 0)),
            pl.BlockSpec((1, HEAD_DIM), lambda b, c: (0, 0)),
        ],
        out_specs=[rowblk, pl.BlockSpec((1, GDN_HEADS, HEAD_DIM, HEAD_DIM), lambda b, c: (b, 0, 0, 0))],
        out_shape=[jax.ShapeDtypeStruct((rows, d), BF16),
                   jax.ShapeDtypeStruct((batch, GDN_HEADS, HEAD_DIM, HEAD_DIM), F32)],
        scratch_shapes=[pltpu.VMEM((GDN_HEADS, HEAD_DIM, HEAD_DIM), F32)],
        compiler_params=_params(2),
        name="gdn_prompt",
    )(q, k, v, proj, gb, out_norm_w.reshape(1, HEAD_DIM))


def _stack_heads(x):
    return jnp.concatenate([x[:, h * HEAD_DIM:(h + 1) * HEAD_DIM] for h in range(GDN_HEADS)], axis=0)


def _gdn_sample_kernel(q_ref, k_ref, v_ref, z_ref, gb_ref, onw_ref, s_in_ref, o_ref, s_out_ref, *, n_seq):
    t = SUBLANES
    n = GDN_HEADS * t
    seqs = range(n_seq)
    heads = range(GDN_HEADS)
    rows = [slice(i * t, (i + 1) * t) for i in seqs]
    qs = [_stack_heads(q_ref[r, :]) for r in rows]
    ks = [_stack_heads(k_ref[r, :]) for r in rows]
    vs = [_stack_heads(v_ref[r, :]) for r in rows]
    gbs = [gb_ref[r, :] for r in rows]
    g_bs = [jnp.concatenate([jnp.broadcast_to(gb[:, h:h + 1], (t, HEAD_DIM)) for h in heads], axis=0) for gb in gbs]
    beta_bs = [jnp.concatenate([jnp.broadcast_to(gb[:, GDN_HEADS + h:GDN_HEADS + h + 1], (t, HEAD_DIM))
                                for h in heads], axis=0) for gb in gbs]
    ri = lax.broadcasted_iota(jnp.int32, (n, n), 0)
    ci = lax.broadcasted_iota(jnp.int32, (n, n), 1)
    cum_mask = jnp.where(((ri >> 3) == (ci >> 3)), jnp.where(ci <= ri, 1.0, 0.0), 0.0)
    gcum_bs = [_dot_exact_lhs(cum_mask, g) for g in g_bs]
    ws, uvs, qes, ps = _gdn_intra(qs, ks, vs, gcum_bs, beta_bs, t)
    glast_bs = [jnp.concatenate([jnp.broadcast_to(g[h * t + t - 1:h * t + t, :], (t, HEAD_DIM)) for h in heads],
                                axis=0) for g in gcum_bs]
    kes = [k * jnp.exp(gl - g) for k, gl, g in zip(ks, glast_bs, gcum_bs)]
    wqes = [jnp.concatenate([w, qe], axis=0) for w, qe in zip(ws, qes)]
    wqs = [[_dot(wqes[i], s_in_ref[i, h]) for h in heads] for i in seqs]
    us = [uvs[i] - jnp.concatenate([wqs[i][h][h * t:(h + 1) * t] for h in heads], axis=0) for i in seqs]
    os_ = [jnp.concatenate([wqs[i][h][n + h * t:n + (h + 1) * t] for h in heads], axis=0) + _dot(ps[i], us[i])
           for i in seqs]
    row = lax.broadcasted_iota(jnp.int32, (n, HEAD_DIM), 0)
    for i in seqs:
        for h in heads:
            u_h = jnp.where((row >> 3) == h, us[i], 0.0)
            s_out_ref[i, h] = jnp.exp(glast_bs[i][h * t:h * t + 1, :]) * s_in_ref[i, h] + _dot_tn(kes[i], u_h)
    for i in seqs:
        og = _gated_out(os_[i], _stack_heads(z_ref[rows[i], :]), onw_ref[...])
        for h in heads:
            o_ref[rows[i], h * HEAD_DIM:(h + 1) * HEAD_DIM] = og[h * t:(h + 1) * t]


def gdn_sample(q, k, v, proj, gb, out_norm_w, state, n_seq, per_step):
    t = SUBLANES
    d = D_MODEL
    rowblk = pl.BlockSpec((per_step * t, d), lambda i: (i, 0))
    sblk = pl.BlockSpec((per_step, GDN_HEADS, HEAD_DIM, HEAD_DIM), lambda i: (i, 0, 0, 0))
    return pl.pallas_call(
        functools.partial(_gdn_sample_kernel, n_seq=per_step),
        grid=(n_seq // per_step,),
        in_specs=[
            rowblk, rowblk, rowblk,
            pl.BlockSpec((per_step * t, d), lambda i: (i, COL_Z // d)),
            pl.BlockSpec((per_step * t, 2 * GDN_HEADS), lambda i: (i, 0)),
            pl.BlockSpec((1, HEAD_DIM), lambda i: (0, 0)),
            sblk,
        ],
        out_specs=[rowblk, sblk],
        out_shape=[jax.ShapeDtypeStruct((n_seq * t, d), F32),
                   jax.ShapeDtypeStruct((n_seq, GDN_HEADS, HEAD_DIM, HEAD_DIM), F32)],
        compiler_params=_params(1),
        name="gdn_sample",
    )(q, k, v, proj, gb, out_norm_w.reshape(1, HEAD_DIM), state)


def _attend(q, load_k, load_v):
    outs = []
    for h in range(X_HEADS):
        s = _dot_nt(q[:, h * X_HEAD_DIM:(h + 1) * X_HEAD_DIM], load_k(h)) * (X_HEAD_DIM ** -0.5)
        e = jnp.exp(s - jnp.max(s, axis=-1, keepdims=True))
        p = e / jnp.sum(e, axis=-1, keepdims=True)
        outs.append(_dot(p, load_v(h)))
    return jnp.concatenate(outs, axis=1)


def _attn_prompt_kernel(q_ref, mk_ref, mv_ref, o_ref):
    head = lambda ref: lambda h: ref[0, :, h * X_HEAD_DIM:(h + 1) * X_HEAD_DIM]
    o_ref[...] = _attend(q_ref[...], head(mk_ref), head(mv_ref)).astype(o_ref.dtype)


def attention_prompt(q, mkv, batch, seq, tm):
    nt = seq // tm
    d = D_MODEL
    return pl.pallas_call(
        _attn_prompt_kernel,
        grid=(batch, nt),
        in_specs=[
            pl.BlockSpec((tm, d), lambda b, t: (b * nt + t, 0)),
            pl.BlockSpec((1, N_MEM, d), lambda b, t: (b, 0, 0)),
            pl.BlockSpec((1, N_MEM, d), lambda b, t: (b, 0, 1)),
        ],
        out_specs=pl.BlockSpec((tm, d), lambda b, t: (b * nt + t, 0)),
        out_shape=jax.ShapeDtypeStruct((batch * seq, d), BF16),
        compiler_params=_params(2),
        name="attention_prompt",
    )(q, mkv, mkv)


def _attn_sample_kernel(q_ref, mk_ref, mv_ref, o_ref, *, n_seq):
    t = SUBLANES
    pairs = [(i, h) for i in range(n_seq) for h in range(X_HEADS)]
    scores = []
    for i, h in pairs:
        q = q_ref[i * t:(i + 1) * t, h * X_HEAD_DIM:(h + 1) * X_HEAD_DIM]
        q16 = jnp.concatenate([q, jnp.zeros_like(q)], axis=0)
        scores.append(_dot_nt(q16, mk_ref[i, :, h, :]) * (X_HEAD_DIM ** -0.5))
    probs = []
    for s in scores:
        e = jnp.exp(s - jnp.max(s, axis=-1, keepdims=True))
        probs.append(e / jnp.sum(e, axis=-1, keepdims=True))
    ctxs = [_dot(p, mv_ref[i, :, h, :]) for p, (i, h) in zip(probs, pairs)]
    for c, (i, h) in zip(ctxs, pairs):
        o_ref[i * t:(i + 1) * t, h * X_HEAD_DIM:(h + 1) * X_HEAD_DIM] = c[:t]


def attention_sample(q, mem_k, mem_v, n_seq, per_step):
    t = SUBLANES
    d = D_MODEL
    return pl.pallas_call(
        functools.partial(_attn_sample_kernel, n_seq=per_step),
        grid=(n_seq // per_step,),
        in_specs=[
            pl.BlockSpec((per_step * t, d), lambda i: (i, 0)),
            pl.BlockSpec((per_step, N_MEM, X_HEADS, X_HEAD_DIM), lambda i: (i, 0, 0, 0)),
            pl.BlockSpec((per_step, N_MEM, X_HEADS, X_HEAD_DIM), lambda i: (i, 0, 0, 0)),
        ],
        out_specs=pl.BlockSpec((per_step * t, d), lambda i: (i, 0)),
        out_shape=jax.ShapeDtypeStruct((n_seq * t, d), F32),
        compiler_params=_params(1),
        name="attention_sample",
    )(q, mem_k, mem_v)


def _layer_tail(x1, xn1, ctx_fn, w, tm, tn):
    hq = matmul_rows(xn1, w["xq"], xn1.dtype, tm, tn)
    ctx = ctx_fn(hq)
    x2, xn2 = matmul_residual_norm(ctx, w["xo"], x1, w["norm_mlp"], BF16, tm)
    hm = matmul_rows(xn2, w["up"], BF16, tm, tn, relu2=True)
    return matmul_residual_final_norm(hm, w["down"], x2, w["norm_f"], tm)


def kernel(x_prompt, x_sample, mem_prompt, cache_mem_k, cache_mem_v, state_gdn_conv, state_gdn, state_sc_conv, norm_mix_w, w_in, gdn_conv_w, gdn_A_log, gdn_dt_bias, gdn_out_norm_w, w_gdn_o, sc_conv_w, w_sc_o, w_mix_out, norm_x_w, mem_norm_w, w_xq, w_xkv, w_xo, norm_mlp_w, w_mlp_up, w_mlp_down, norm_f_w):
    batch, seq, d = x_prompt.shape
    n_seq, t_dec, _ = x_sample.shape
    assert norm_mix_w.shape[0] == 1 and t_dec == SUBLANES and d == D_MODEL
    ab0 = 4 * D_MODEL
    w_in0 = w_in[0]
    w = {
        "main": jnp.concatenate([w_in0[:, :ab0], w_in0[:, ab0 + 2 * GDN_HEADS:]], axis=1).astype(BF16),
        "ab": w_in0[:, ab0:ab0 + 2 * GDN_HEADS],
        "gdn_o": w_gdn_o[0].astype(BF16), "sc_o": w_sc_o[0].astype(BF16), "mix": w_mix_out[0].astype(BF16),
        "xq": w_xq[0].astype(BF16), "xkv": w_xkv[0].astype(BF16), "xo": w_xo[0].astype(BF16),
        "up": w_mlp_up[0].astype(BF16), "down": w_mlp_down[0].astype(BF16),
        "norm_mlp": norm_mlp_w[0], "norm_f": norm_f_w,
    }
    tm, tn = 512, 1024

    rows_p = batch * seq
    xp = x_prompt.reshape(rows_p, d)
    memn = rmsnorm_rows(mem_prompt.reshape(batch * N_MEM, d), mem_norm_w[0], tm)
    mkv = matmul_rows(memn, w["xkv"], F32, tm, tn)
    xn, gb = rmsnorm_gates_rows(xp, norm_mix_w[0], w["ab"], gdn_A_log[0], gdn_dt_bias[0], tm)
    proj = matmul_rows(xn, w["main"], BF16, tm, tn)
    q, k, v, u_sc, cx_tail = prep_prompt(proj, gdn_conv_w[0], sc_conv_w[0], batch, seq, 256)
    o_gdn, s_prompt = gdn_prompt(q, k, v, proj, gb, gdn_out_norm_w[0], batch, seq)
    x1, xn1 = merge_mixers(o_gdn, u_sc, proj, xp, w["gdn_o"], w["sc_o"], w["mix"], norm_x_w[0], tm)
    mkv3 = mkv.reshape(batch, N_MEM, 2 * d)
    y_prompt = _layer_tail(x1, xn1, lambda hq: attention_prompt(hq, mkv3, batch, seq, tm), w, tm, tn)
    p_conv = proj.reshape(batch, seq, MAIN_COLS)[:, seq - (GDN_CONV_W - 1):, :3 * d].astype(F32)
    p_sc = cx_tail[:, SUBLANES - (SC_CONV_W - 1):]
    p_mk = mkv3[:, :, :d].reshape(batch, N_MEM, X_HEADS, X_HEAD_DIM)
    p_mv = mkv3[:, :, d:].reshape(batch, N_MEM, X_HEADS, X_HEAD_DIM)

    rows_s = n_seq * t_dec
    xs = x_sample.reshape(rows_s, d)
    xn_s, gb_s = rmsnorm_gates_rows(xs, norm_mix_w[0], w["ab"], gdn_A_log[0], gdn_dt_bias[0], tm)
    proj_s = matmul_rows(xn_s, w["main"], F32, tm, tn)
    qst8 = jnp.pad(state_gdn_conv[0], ((0, 0), (0, SUBLANES - (GDN_CONV_W - 1)), (0, 0))).reshape(rows_s, 3 * d)
    cst8 = jnp.pad(state_sc_conv[0], ((0, 0), (0, SUBLANES - (SC_CONV_W - 1)), (0, 0))).reshape(rows_s, d)
    q_s, k_s, v_s, u_s, cx_s = prep_sample(proj_s, qst8, cst8, gdn_conv_w[0], sc_conv_w[0], n_seq, 16)
    o_s, s_sample = gdn_sample(q_s, k_s, v_s, proj_s, gb_s, gdn_out_norm_w[0], state_gdn[0], n_seq, 4)
    x1_s, xn1_s = merge_mixers(o_s, u_s, proj_s, xs, w["gdn_o"], w["sc_o"], w["mix"], norm_x_w[0], tm)
    y_sample = _layer_tail(
        x1_s, xn1_s, lambda hq: attention_sample(hq.astype(F32), cache_mem_k[0], cache_mem_v[0], n_seq, 4), w, tm, tn)
    s_conv = proj_s.reshape(n_seq, t_dec, MAIN_COLS)[:, t_dec - (GDN_CONV_W - 1):, :3 * d]
    s_sc = cx_s.reshape(n_seq, t_dec, d)[:, t_dec - (SC_CONV_W - 1):]

    return (y_prompt.reshape(batch, seq, d), y_sample.reshape(n_seq, t_dec, d),
            p_mk[None], p_mv[None], p_conv[None], s_prompt[None], p_sc[None],
            s_conv[None], s_sample[None], s_sc[None])
```

```python
import functools

import jax
import jax.numpy as jnp
from jax import lax
from jax.experimental import pallas as pl
from jax.experimental.pallas import tpu as pltpu

F32 = jnp.float32
BF16 = jnp.bfloat16

D_MODEL = 1024
GDN_HEADS = 8
HEAD_DIM = 128
GDN_CHUNK = 64
GDN_CONV_W = 4
SC_CONV_W = 3
X_HEADS = 4
X_HEAD_DIM = 256
N_MEM = 256
D_FF = 4096
EPS = 1e-6
SUBLANES = 8
VMEM_LIMIT = 48 * 1024 * 1024

COL_Z = 3 * D_MODEL
COL_BG = 4 * D_MODEL
COL_CG = 5 * D_MODEL
COL_XIN = 6 * D_MODEL
COL_GA = 7 * D_MODEL
COL_GB = 8 * D_MODEL
MAIN_COLS = 9 * D_MODEL


def _params(n_axes):
    return pltpu.CompilerParams(dimension_semantics=("arbitrary",) * n_axes, vmem_limit_bytes=VMEM_LIMIT)


def _dot(a, b):
    return jnp.dot(a.astype(BF16), b.astype(BF16), preferred_element_type=F32)


def _dot_nt(a, b):
    return lax.dot_general(a.astype(BF16), b.astype(BF16), (((1,), (1,)), ((), ())), preferred_element_type=F32)


def _dot_tn(a, b):
    return lax.dot_general(a.astype(BF16), b.astype(BF16), (((0,), (0,)), ((), ())), preferred_element_type=F32)


def _split_hi_lo(x):
    hi = x.astype(BF16)
    lo = (x - hi.astype(F32)).astype(BF16)
    return hi, lo


def _dot_exact_lhs(m01, x):
    hi, lo = _split_hi_lo(x)
    m = m01.astype(BF16)
    return jnp.dot(m, hi, preferred_element_type=F32) + jnp.dot(m, lo, preferred_element_type=F32)


def _rms(x, w):
    return x * lax.rsqrt(jnp.mean(x * x, axis=-1, keepdims=True) + EPS) * w


def _sigmoid(x):
    return 1.0 / (1.0 + jnp.exp(-x))


def _silu(x):
    return x * _sigmoid(x)


def _softplus(x):
    return jnp.maximum(x, 0.0) + jnp.log(1.0 + jnp.exp(-jnp.abs(x)))


def _norm_kernel(x_ref, w_ref, o_ref):
    o_ref[...] = _rms(x_ref[...], w_ref[...]).astype(o_ref.dtype)


def rmsnorm_rows(x, w, tm):
    rows, d = x.shape
    return pl.pallas_call(
        _norm_kernel,
        grid=(rows // tm,),
        in_specs=[pl.BlockSpec((tm, d), lambda i: (i, 0)), pl.BlockSpec((1, d), lambda i: (0, 0))],
        out_specs=pl.BlockSpec((tm, d), lambda i: (i, 0)),
        out_shape=jax.ShapeDtypeStruct((rows, d), BF16),
        compiler_params=_params(1),
        name="rmsnorm_rows",
    )(x, w.reshape(1, d))


def _norm_gates_kernel(x_ref, w_ref, wab_ref, alog_ref, dtb_ref, o_ref, gb_ref):
    xn = _rms(x_ref[...], w_ref[...]).astype(BF16)
    o_ref[...] = xn
    ab = jnp.dot(xn, wab_ref[...], preferred_element_type=F32)
    a = ab[:, :GDN_HEADS]
    b = ab[:, GDN_HEADS:2 * GDN_HEADS]
    g = -jnp.exp(alog_ref[...]) * _softplus(a + dtb_ref[...])
    gb_ref[...] = jnp.concatenate([g, _sigmoid(b)], axis=1)


def rmsnorm_gates_rows(x, w, w_ab, a_log, dt_bias, tm):
    rows, d = x.shape
    wab = jnp.zeros((d, 128), BF16).at[:, :2 * GDN_HEADS].set(w_ab.astype(BF16))
    return pl.pallas_call(
        _norm_gates_kernel,
        grid=(rows // tm,),
        in_specs=[
            pl.BlockSpec((tm, d), lambda i: (i, 0)),
            pl.BlockSpec((1, d), lambda i: (0, 0)),
            pl.BlockSpec((d, 128), lambda i: (0, 0)),
            pl.BlockSpec((1, GDN_HEADS), lambda i: (0, 0)),
            pl.BlockSpec((1, GDN_HEADS), lambda i: (0, 0)),
        ],
        out_specs=[pl.BlockSpec((tm, d), lambda i: (i, 0)), pl.BlockSpec((tm, 2 * GDN_HEADS), lambda i: (i, 0))],
        out_shape=[jax.ShapeDtypeStruct((rows, d), BF16), jax.ShapeDtypeStruct((rows, 2 * GDN_HEADS), F32)],
        compiler_params=_params(1),
        name="rmsnorm_gates_rows",
    )(x, w.reshape(1, d), wab, a_log.reshape(1, GDN_HEADS), dt_bias.reshape(1, GDN_HEADS))


def _mm_kernel(a_ref, w_ref, o_ref, *, relu2):
    acc = _dot(a_ref[...], w_ref[...])
    if relu2:
        acc = jnp.square(jnp.maximum(acc, 0.0))
    o_ref[...] = acc.astype(o_ref.dtype)


def matmul_rows(a, w, out_dtype, tm, tn, relu2=False):
    rows, k = a.shape
    n = w.shape[1]
    return pl.pallas_call(
        functools.partial(_mm_kernel, relu2=relu2),
        grid=(n // tn, rows // tm),
        in_specs=[pl.BlockSpec((tm, k), lambda j, i: (i, 0)), pl.BlockSpec((k, tn), lambda j, i: (0, j))],
        out_specs=pl.BlockSpec((tm, tn), lambda j, i: (i, j)),
        out_shape=jax.ShapeDtypeStruct((rows, n), out_dtype),
        compiler_params=_params(2),
        name="matmul_rows",
    )(a, w)


def _mm_res_norm_kernel(a_ref, w_ref, res_ref, nw_ref, x_ref, xn_ref):
    x = res_ref[...] + _dot(a_ref[...], w_ref[...])
    x_ref[...] = x
    xn_ref[...] = _rms(x, nw_ref[...]).astype(xn_ref.dtype)


def matmul_residual_norm(a, w, res, norm_w, xn_dtype, tm):
    rows, k = a.shape
    n = w.shape[1]
    return pl.pallas_call(
        _mm_res_norm_kernel,
        grid=(rows // tm,),
        in_specs=[
            pl.BlockSpec((tm, k), lambda i: (i, 0)),
            pl.BlockSpec((k, n), lambda i: (0, 0)),
            pl.BlockSpec((tm, n), lambda i: (i, 0)),
            pl.BlockSpec((1, n), lambda i: (0, 0)),
        ],
        out_specs=[pl.BlockSpec((tm, n), lambda i: (i, 0)), pl.BlockSpec((tm, n), lambda i: (i, 0))],
        out_shape=[jax.ShapeDtypeStruct((rows, n), F32), jax.ShapeDtypeStruct((rows, n), xn_dtype)],
        compiler_params=_params(1),
        name="matmul_residual_norm",
    )(a, w, res, norm_w.reshape(1, n))


def _merge_kernel(oa_ref, ub_ref, ga_ref, gb_ref, x_ref, wa_ref, wb_ref, wm_ref, nw_ref, x1_ref, xn_ref):
    ya = _dot(oa_ref[...], wa_ref[...])
    yb = _dot(ub_ref[...], wb_ref[...])
    merged = _sigmoid(ga_ref[...].astype(F32)) * ya + _sigmoid(gb_ref[...].astype(F32)) * yb
    x1 = x_ref[...] + _dot(merged, wm_ref[...])
    x1_ref[...] = x1
    xn_ref[...] = _rms(x1, nw_ref[...]).astype(xn_ref.dtype)


def merge_mixers(o_a, u_b, proj, x, w_gdn_o, w_sc_o, w_mix_out, norm_w, tm):
    rows, d = x.shape
    row_blk = lambda i: (i, 0)
    full = lambda i: (0, 0)
    return pl.pallas_call(
        _merge_kernel,
        grid=(rows // tm,),
        in_specs=[
            pl.BlockSpec((tm, d), row_blk),
            pl.BlockSpec((tm, d), row_blk),
            pl.BlockSpec((tm, d), lambda i: (i, COL_GA // D_MODEL)),
            pl.BlockSpec((tm, d), lambda i: (i, COL_GB // D_MODEL)),
            pl.BlockSpec((tm, d), row_blk),
            pl.BlockSpec((d, d), full),
            pl.BlockSpec((d, d), full),
            pl.BlockSpec((d, d), full),
            pl.BlockSpec((1, d), full),
        ],
        out_specs=[pl.BlockSpec((tm, d), row_blk), pl.BlockSpec((tm, d), row_blk)],
        out_shape=[jax.ShapeDtypeStruct((rows, d), F32), jax.ShapeDtypeStruct((rows, d), BF16)],
        compiler_params=_params(1),
        name="merge_mixers",
    )(o_a, u_b, proj, proj, x, w_gdn_o, w_sc_o, w_mix_out, norm_w.reshape(1, d))


def _mm_res_final_kernel(a_ref, w_ref, res_ref, nw_ref, y_ref):
    x = res_ref[...] + _dot(a_ref[...], w_ref[...])
    y_ref[...] = _rms(x, nw_ref[...])


def matmul_residual_final_norm(a, w, res, norm_w, tm):
    rows, k = a.shape
    n = w.shape[1]
    return pl.pallas_call(
        _mm_res_final_kernel,
        grid=(rows // tm,),
        in_specs=[
            pl.BlockSpec((tm, k), lambda i: (i, 0)),
            pl.BlockSpec((k, n), lambda i: (0, 0)),
            pl.BlockSpec((tm, n), lambda i: (i, 0)),
            pl.BlockSpec((1, n), lambda i: (0, 0)),
        ],
        out_specs=pl.BlockSpec((tm, n), lambda i: (i, 0)),
        out_shape=jax.ShapeDtypeStruct((rows, n), F32),
        compiler_params=_params(1),
        name="matmul_residual_final_norm",
    )(a, w, res, norm_w.reshape(1, n))


def _shifted(x, prev_tail, s):
    rolled = pltpu.roll(x, s, 0)
    head = pltpu.roll(prev_tail, s, 0)
    row = lax.broadcasted_iota(jnp.int32, head.shape, 0)
    first = jnp.where(row < s, head, rolled[:SUBLANES])
    if x.shape[0] == SUBLANES:
        return first
    return jnp.concatenate([first, rolled[SUBLANES:]], axis=0)


def _shifted_groups(x, hist, s, n_hist):
    rows = x.shape[0]
    rolled = pltpu.roll(x, s, 0)
    head = hist if n_hist == s else pltpu.roll(hist, rows - (n_hist - s), 0)
    row = lax.broadcasted_iota(jnp.int32, x.shape, 0)
    return jnp.where((row & (SUBLANES - 1)) < s, head, rolled)


def _causal_conv(x, w, shift_fn):
    width = w.shape[0]
    y = x * w[width - 1:width]
    for i in range(width - 1):
        y = y + shift_fn(width - 1 - i) * w[i:i + 1]
    return y


def _l2norm_heads(x, scale):
    outs = []
    for h in range(GDN_HEADS):
        xh = x[:, h * HEAD_DIM:(h + 1) * HEAD_DIM]
        outs.append(xh * (lax.rsqrt(jnp.sum(xh * xh, axis=-1, keepdims=True) + EPS) * scale))
    return jnp.concatenate(outs, axis=1)


def _prep_body(qkv, bg, cg, xin, cw, sw, qkv_shift, cx_shift):
    conv = _silu(_causal_conv(qkv, cw, qkv_shift))
    q = _l2norm_heads(conv[:, :D_MODEL], HEAD_DIM ** -0.5)
    k = _l2norm_heads(conv[:, D_MODEL:2 * D_MODEL], 1.0)
    v = conv[:, 2 * D_MODEL:]
    cx = cg * xin
    u = bg * _causal_conv(cx, sw, functools.partial(cx_shift, cx))
    return q, k, v, u, cx


def _prep_prompt_kernel(qkv_ref, bg_ref, cg_ref, xin_ref, cw_ref, sw_ref,
                        q_ref, k_ref, v_ref, u_ref, cxt_ref, qkv_tail, cx_tail):
    @pl.when(pl.program_id(1) == 0)
    def _():
        qkv_tail[...] = jnp.zeros_like(qkv_tail)
        cx_tail[...] = jnp.zeros_like(cx_tail)

    qkv = qkv_ref[...].astype(F32)
    q, k, v, u, cx = _prep_body(qkv, bg_ref[...].astype(F32), cg_ref[...].astype(F32), xin_ref[...].astype(F32),
                                cw_ref[...], sw_ref[...],
                                lambda s: _shifted(qkv, qkv_tail[...], s),
                                lambda cx, s: _shifted(cx, cx_tail[...], s))
    q_ref[...] = q.astype(q_ref.dtype)
    k_ref[...] = k.astype(k_ref.dtype)
    v_ref[...] = v.astype(v_ref.dtype)
    u_ref[...] = u.astype(u_ref.dtype)
    tm = qkv.shape[0]
    qkv_tail[...] = qkv[tm - SUBLANES:]
    cx_tail[...] = cx[tm - SUBLANES:]
    cxt_ref[0] = cx[tm - SUBLANES:]


def prep_prompt(proj, conv_w, sc_w, batch, seq, tm):
    rows = batch * seq
    nt = seq // tm
    d = D_MODEL
    blk = lambda c: pl.BlockSpec((tm, d), lambda b, t, c=c: (b * nt + t, c))
    out_rows = pl.BlockSpec((tm, d), lambda b, t: (b * nt + t, 0))
    return pl.pallas_call(
        _prep_prompt_kernel,
        grid=(batch, nt),
        in_specs=[
            pl.BlockSpec((tm, 3 * d), lambda b, t: (b * nt + t, 0)),
            blk(COL_BG // d), blk(COL_CG // d), blk(COL_XIN // d),
            pl.BlockSpec((GDN_CONV_W, 3 * d), lambda b, t: (0, 0)),
            pl.BlockSpec((SC_CONV_W, d), lambda b, t: (0, 0)),
        ],
        out_specs=[out_rows, out_rows, out_rows, out_rows, pl.BlockSpec((1, SUBLANES, d), lambda b, t: (b, 0, 0))],
        out_shape=[jax.ShapeDtypeStruct((rows, d), BF16)] * 4 + [jax.ShapeDtypeStruct((batch, SUBLANES, d), F32)],
        scratch_shapes=[pltpu.VMEM((SUBLANES, 3 * d), F32), pltpu.VMEM((SUBLANES, d), F32)],
        compiler_params=_params(2),
        name="prep_prompt",
    )(proj, proj, proj, proj, conv_w, sc_w)


def _prep_sample_kernel(qkv_ref, bg_ref, cg_ref, xin_ref, qst_ref, cst_ref, cw_ref, sw_ref,
                        q_ref, k_ref, v_ref, u_ref, cx_ref):
    qkv = qkv_ref[...]
    q, k, v, u, cx = _prep_body(qkv, bg_ref[...], cg_ref[...], xin_ref[...], cw_ref[...], sw_ref[...],
                                lambda s: _shifted_groups(qkv, qst_ref[...], s, GDN_CONV_W - 1),
                                lambda cx, s: _shifted_groups(cx, cst_ref[...], s, SC_CONV_W - 1))
    q_ref[...] = q
    k_ref[...] = k
    v_ref[...] = v
    u_ref[...] = u
    cx_ref[...] = cx


def prep_sample(proj, qkv_state8, cx_state8, conv_w, sc_w, n_seq, per_step):
    rows = n_seq * SUBLANES
    d = D_MODEL
    t = SUBLANES * per_step
    blk = lambda c: pl.BlockSpec((t, d), lambda n, c=c: (n, c))
    out_rows = pl.BlockSpec((t, d), lambda n: (n, 0))
    return pl.pallas_call(
        _prep_sample_kernel,
        grid=(n_seq // per_step,),
        in_specs=[
            pl.BlockSpec((t, 3 * d), lambda n: (n, 0)),
            blk(COL_BG // d), blk(COL_CG // d), blk(COL_XIN // d),
            pl.BlockSpec((t, 3 * d), lambda n: (n, 0)),
            pl.BlockSpec((t, d), lambda n: (n, 0)),
            pl.BlockSpec((GDN_CONV_W, 3 * d), lambda n: (0, 0)),
            pl.BlockSpec((SC_CONV_W, d), lambda n: (0, 0)),
        ],
        out_specs=[out_rows] * 5,
        out_shape=[jax.ShapeDtypeStruct((rows, d), F32)] * 5,
        compiler_params=_params(1),
        name="prep_sample",
    )(proj, proj, proj, proj, qkv_state8, cx_state8, conv_w, sc_w)


def _tri_inverse(l_mats, n, top):
    ri = lax.broadcasted_iota(jnp.int32, (n, n), 0)
    ci = lax.broadcasted_iota(jnp.int32, (n, n), 1)
    eye = jnp.where(ri == ci, 1.0, 0.0)
    ts = [eye - jnp.where((ri ^ ci) == 1, l, 0.0) for l in l_mats]
    shift = 1
    while (1 << shift) < top:
        pair = ((ri >> shift) ^ (ci >> shift)) == 1
        xs = [_dot(jnp.where(pair, l, 0.0), t) for l, t in zip(l_mats, ts)]
        ts = [t - _dot(t, x) for t, x in zip(ts, xs)]
        shift += 1
    return ts


def _pair_diff(col_b):
    hi, lo = _split_hi_lo(col_b)
    hi = hi.astype(F32)
    lo = lo.astype(F32)
    lane = lax.broadcasted_iota(jnp.int32, col_b.shape, 1)
    left = jnp.where(lane == 0, hi, jnp.where(lane == 1, lo, jnp.where(lane < 4, 1.0, 0.0)))
    right = jnp.where(lane == 2, -hi, jnp.where(lane == 3, -lo, jnp.where(lane < 2, 1.0, 0.0)))
    return _dot_nt(left, right)


def _gdn_intra(qs, ks, vs, gcum_bs, beta_bs, block):
    n = qs[0].shape[0]
    ri = lax.broadcasted_iota(jnp.int32, (n, n), 0)
    ci = lax.broadcasted_iota(jnp.int32, (n, n), 1)
    same = (ri >> (block.bit_length() - 1)) == (ci >> (block.bit_length() - 1))
    diffs = [_pair_diff(g) for g in gcum_bs]
    decays = [jnp.where(same, jnp.where(ci <= ri, jnp.exp(jnp.minimum(d, 0.0)), 0.0), 0.0) for d in diffs]
    kks = [_dot_nt(k, k) for k in ks]
    l_mats = [b[:, :n] * kk * jnp.where(ci < ri, dec, 0.0) for b, kk, dec in zip(beta_bs, kks, decays)]
    ts = _tri_inverse(l_mats, n, block)
    egcs = [jnp.exp(g) for g in gcum_bs]
    wus = [_dot(t, jnp.concatenate([k * (b * e), v * b], axis=1))
           for t, k, v, b, e in zip(ts, ks, vs, beta_bs, egcs)]
    ps = [_dot_nt(q, k) * dec for q, k, dec in zip(qs, ks, decays)]
    return ([wu[:, :HEAD_DIM] for wu in wus], [wu[:, HEAD_DIM:] for wu in wus],
            [q * e for q, e in zip(qs, egcs)], ps)


def _gated_out(o, z, onw):
    return _rms(o, onw) * _silu(z)


def _gdn_prompt_kernel(q_ref, k_ref, v_ref, z_ref, gb_ref, onw_ref, o_ref, s_out_ref, s_ref):
    c = pl.program_id(1)

    @pl.when(c == 0)
    def _():
        s_ref[...] = jnp.zeros_like(s_ref)

    n = GDN_CHUNK
    gb = gb_ref[...]
    ri = lax.broadcasted_iota(jnp.int32, (n, n), 0)
    ci = lax.broadcasted_iota(jnp.int32, (n, n), 1)
    gcum = _dot_exact_lhs(jnp.where(ci <= ri, 1.0, 0.0), gb)
    heads = range(GDN_HEADS)
    sls = [slice(h * HEAD_DIM, (h + 1) * HEAD_DIM) for h in heads]
    qs = [q_ref[:, sl].astype(F32) for sl in sls]
    ks = [k_ref[:, sl].astype(F32) for sl in sls]
    vs = [v_ref[:, sl].astype(F32) for sl in sls]
    gcum_bs = [jnp.broadcast_to(gcum[:, h:h + 1], (n, HEAD_DIM)) for h in heads]
    beta_bs = [jnp.broadcast_to(gb[:, GDN_HEADS + h:GDN_HEADS + h + 1], (n, HEAD_DIM)) for h in heads]
    ws, uvs, qes, ps = _gdn_intra(qs, ks, vs, gcum_bs, beta_bs, n)
    ss = [s_ref[h] for h in heads]
    wqs = [_dot(jnp.concatenate([w, qe], axis=0), s) for w, qe, s in zip(ws, qes, ss)]
    us = [uv - wq[:n] for uv, wq in zip(uvs, wqs)]
    os_ = [wq[n:] + _dot(p, u) for wq, p, u in zip(wqs, ps, us)]
    glasts = [g[n - 1:n, :] for g in gcum_bs]
    kes = [k * jnp.exp(gl - g) for k, gl, g in zip(ks, glasts, gcum_bs)]
    s_news = [jnp.exp(gl) * s + _dot_tn(ke, u) for gl, s, ke, u in zip(glasts, ss, kes, us)]
    for h in heads:
        s_ref[h] = s_news[h]
        o_ref[:, sls[h]] = _gated_out(os_[h], z_ref[:, sls[h]].astype(F32), onw_ref[...]).astype(o_ref.dtype)

    @pl.when(c == pl.num_programs(1) - 1)
    def _():
        s_out_ref[0] = s_ref[...]


def gdn_prompt(q, k, v, proj, gb, out_norm_w, batch, seq):
    rows = batch * seq
    n = GDN_CHUNK
    nc = seq // n
    d = D_MODEL
    rowblk = pl.BlockSpec((n, d), lambda b, c: (b * nc + c, 0))
    return pl.pallas_call(
        _gdn_prompt_kernel,
        grid=(batch, nc),
        in_specs=[
            rowblk, rowblk, rowblk,
            pl.BlockSpec((n, d), lambda b, c: (b * nc + c, COL_Z // d)),
            pl.BlockSpec((n, 2 * GDN_HEADS), lambda b, c: (b * nc + c, 0)),
            pl.BlockSpec((1, HEAD_DIM), lambda b, c: (0, 0)),
        ],
        out_specs=[rowblk, pl.BlockSpec((1, GDN_HEADS, HEAD_DIM, HEAD_DIM), lambda b, c: (b, 0, 0, 0))],
        out_shape=[jax.ShapeDtypeStruct((rows, d), BF16),
                   jax.ShapeDtypeStruct((batch, GDN_HEADS, HEAD_DIM, HEAD_DIM), F32)],
        scratch_shapes=[pltpu.VMEM((GDN_HEADS, HEAD_DIM, HEAD_DIM), F32)],
        compiler_params=_params(2),
        name="gdn_prompt",
    )(q, k, v, proj, gb, out_norm_w.reshape(1, HEAD_DIM))


def _stack_heads(x):
    return jnp.concatenate([x[:, h * HEAD_DIM:(h + 1) * HEAD_DIM] for h in range(GDN_HEADS)], axis=0)


def _gdn_sample_kernel(q_ref, k_ref, v_ref, z_ref, gb_ref, onw_ref, s_in_ref, o_ref, s_out_ref, *, n_seq):
    t = SUBLANES
    n = GDN_HEADS * t
    seqs = range(n_seq)
    heads = range(GDN_HEADS)
    rows = [slice(i * t, (i + 1) * t) for i in seqs]
    qs = [_stack_heads(q_ref[r, :]) for r in rows]
    ks = [_stack_heads(k_ref[r, :]) for r in rows]
    vs = [_stack_heads(v_ref[r, :]) for r in rows]
    gbs = [gb_ref[r, :] for r in rows]
    g_bs = [jnp.concatenate([jnp.broadcast_to(gb[:, h:h + 1], (t, HEAD_DIM)) for h in heads], axis=0) for gb in gbs]
    beta_bs = [jnp.concatenate([jnp.broadcast_to(gb[:, GDN_HEADS + h:GDN_HEADS + h + 1], (t, HEAD_DIM))
                                for h in heads], axis=0) for gb in gbs]
    ri = lax.broadcasted_iota(jnp.int32, (n, n), 0)
    ci = lax.broadcasted_iota(jnp.int32, (n, n), 1)
    cum_mask = jnp.where(((ri >> 3) == (ci >> 3)), jnp.where(ci <= ri, 1.0, 0.0), 0.0)
    gcum_bs = [_dot_exact_lhs(cum_mask, g) for g in g_bs]
    ws, uvs, qes, ps = _gdn_intra(qs, ks, vs, gcum_bs, beta_bs, t)
    glast_bs = [jnp.concatenate([jnp.broadcast_to(g[h * t + t - 1:h * t + t, :], (t, HEAD_DIM)) for h in heads],
                                axis=0) for g in gcum_bs]
    kes = [k * jnp.exp(gl - g) for k, gl, g in zip(ks, glast_bs, gcum_bs)]
    wqes = [jnp.concatenate([w, qe], axis=0) for w, qe in zip(ws, qes)]
    wqs = [[_dot(wqes[i], s_in_ref[i, h]) for h in heads] for i in seqs]
    us = [uvs[i] - jnp.concatenate([wqs[i][h][h * t:(h + 1) * t] for h in heads], axis=0) for i in seqs]
    os_ = [jnp.concatenate([wqs[i][h][n + h * t:n + (h + 1) * t] for h in heads], axis=0) + _dot(ps[i], us[i])
           for i in seqs]
    row = lax.broadcasted_iota(jnp.int32, (n, HEAD_DIM), 0)
    for i in seqs:
        for h in heads:
            u_h = jnp.where((row >> 3) == h, us[i], 0.0)
            s_out_ref[i, h] = jnp.exp(glast_bs[i][h * t:h * t + 1, :]) * s_in_ref[i, h] + _dot_tn(kes[i], u_h)
    for i in seqs:
        og = _gated_out(os_[i], _stack_heads(z_ref[rows[i], :]), onw_ref[...])
        for h in heads:
            o_ref[rows[i], h * HEAD_DIM:(h + 1) * HEAD_DIM] = og[h * t:(h + 1) * t]


def gdn_sample(q, k, v, proj, gb, out_norm_w, state, n_seq, per_step):
    t = SUBLANES
    d = D_MODEL
    rowblk = pl.BlockSpec((per_step * t, d), lambda i: (i, 0))
    sblk = pl.BlockSpec((per_step, GDN_HEADS, HEAD_DIM, HEAD_DIM), lambda i: (i, 0, 0, 0))
    return pl.pallas_call(
        functools.partial(_gdn_sample_kernel, n_seq=per_step),
        grid=(n_seq // per_step,),
        in_specs=[
            rowblk, rowblk, rowblk,
            pl.BlockSpec((per_step * t, d), lambda i: (i, COL_Z // d)),
            pl.BlockSpec((per_step * t, 2 * GDN_HEADS), lambda i: (i, 0)),
            pl.BlockSpec((1, HEAD_DIM), lambda i: (0, 0)),
            sblk,
        ],
        out_specs=[rowblk, sblk],
        out_shape=[jax.ShapeDtypeStruct((n_seq * t, d), F32),
                   jax.ShapeDtypeStruct((n_seq, GDN_HEADS, HEAD_DIM, HEAD_DIM), F32)],
        compiler_params=_params(1),
        name="gdn_sample",
    )(q, k, v, proj, gb, out_norm_w.reshape(1, HEAD_DIM), state)


def _attend(q, load_k, load_v):
    outs = []
    for h in range(X_HEADS):
        s = _dot_nt(q[:, h * X_HEAD_DIM:(h + 1) * X_HEAD_DIM], load_k(h)) * (X_HEAD_DIM ** -0.5)
        e = jnp.exp(s - jnp.max(s, axis=-1, keepdims=True))
        p = e / jnp.sum(e, axis=-1, keepdims=True)
        outs.append(_dot(p, load_v(h)))
    return jnp.concatenate(outs, axis=1)


def _attn_prompt_kernel(q_ref, mk_ref, mv_ref, o_ref):
    head = lambda ref: lambda h: ref[0, :, h * X_HEAD_DIM:(h + 1) * X_HEAD_DIM]
    o_ref[...] = _attend(q_ref[...], head(mk_ref), head(mv_ref)).astype(o_ref.dtype)


def attention_prompt(q, mkv, batch, seq, tm):
    nt = seq // tm
    d = D_MODEL
    return pl.pallas_call(
        _attn_prompt_kernel,
        grid=(batch, nt),
        in_specs=[
            pl.BlockSpec((tm, d), lambda b, t: (b * nt + t, 0)),
            pl.BlockSpec((1, N_MEM, d), lambda b, t: (b, 0, 0)),
            pl.BlockSpec((1, N_MEM, d), lambda b, t: (b, 0, 1)),
        ],
        out_specs=pl.BlockSpec((tm, d), lambda b, t: (b * nt + t, 0)),
        out_shape=jax.ShapeDtypeStruct((batch * seq, d), BF16),
        compiler_params=_params(2),
        name="attention_prompt",
    )(q, mkv, mkv)


def _attn_sample_kernel(q_ref, mk_ref, mv_ref, o_ref, *, n_seq):
    t = SUBLANES
    n_rows = N_MEM * X_HEADS
    seqs = range(n_seq)
    heads = range(X_HEADS)
    scores = []
    for i in seqs:
        q = q_ref[i * t:(i + 1) * t, :]
        q_all = jnp.concatenate([q[:, h * X_HEAD_DIM:(h + 1) * X_HEAD_DIM] for h in heads], axis=0)
        scores.append(_dot_nt(q_all, mk_ref[i].reshape(n_rows, X_HEAD_DIM)) * (X_HEAD_DIM ** -0.5))
    row_head = lax.broadcasted_iota(jnp.int32, (X_HEADS * t, n_rows), 0) >> 3
    col_head = lax.broadcasted_iota(jnp.int32, (X_HEADS * t, n_rows), 1) & (X_HEADS - 1)
    same_head = row_head == col_head
    probs = []
    for s in scores:
        s = jnp.where(same_head, s, -jnp.inf)
        e = jnp.exp(s - jnp.max(s, axis=-1, keepdims=True))
        probs.append(e / jnp.sum(e, axis=-1, keepdims=True))
    ctxs = [_dot(p, mv_ref[i].reshape(n_rows, X_HEAD_DIM)) for p, i in zip(probs, seqs)]
    for c, i in zip(ctxs, seqs):
        for h in heads:
            o_ref[i * t:(i + 1) * t, h * X_HEAD_DIM:(h + 1) * X_HEAD_DIM] = c[h * t:(h + 1) * t]


def attention_sample(q, mem_k, mem_v, n_seq, per_step):
    t = SUBLANES
    d = D_MODEL
    return pl.pallas_call(
        functools.partial(_attn_sample_kernel, n_seq=per_step),
        grid=(n_seq // per_step,),
        in_specs=[
            pl.BlockSpec((per_step * t, d), lambda i: (i, 0)),
            pl.BlockSpec((per_step, N_MEM, X_HEADS, X_HEAD_DIM), lambda i: (i, 0, 0, 0)),
            pl.BlockSpec((per_step, N_MEM, X_HEADS, X_HEAD_DIM), lambda i: (i, 0, 0, 0)),
        ],
        out_specs=pl.BlockSpec((per_step * t, d), lambda i: (i, 0)),
        out_shape=jax.ShapeDtypeStruct((n_seq * t, d), F32),
        compiler_params=_params(1),
        name="attention_sample",
    )(q, mem_k, mem_v)


def _layer_tail(x1, xn1, ctx_fn, w, tm, tn):
    hq = matmul_rows(xn1, w["xq"], xn1.dtype, tm, tn)
    ctx = ctx_fn(hq)
    x2, xn2 = matmul_residual_norm(ctx, w["xo"], x1, w["norm_mlp"], BF16, tm)
    hm = matmul_rows(xn2, w["up"], BF16, tm, tn, relu2=True)
    return matmul_residual_final_norm(hm, w["down"], x2, w["norm_f"], tm)


def kernel(x_prompt, x_sample, mem_prompt, cache_mem_k, cache_mem_v, state_gdn_conv, state_gdn, state_sc_conv, norm_mix_w, w_in, gdn_conv_w, gdn_A_log, gdn_dt_bias, gdn_out_norm_w, w_gdn_o, sc_conv_w, w_sc_o, w_mix_out, norm_x_w, mem_norm_w, w_xq, w_xkv, w_xo, norm_mlp_w, w_mlp_up, w_mlp_down, norm_f_w):
    batch, seq, d = x_prompt.shape
    n_seq, t_dec, _ = x_sample.shape
    assert norm_mix_w.shape[0] == 1 and t_dec == SUBLANES and d == D_MODEL
    ab0 = 4 * D_MODEL
    w_in0 = w_in[0]
    w = {
        "main": jnp.concatenate([w_in0[:, :ab0], w_in0[:, ab0 + 2 * GDN_HEADS:]], axis=1).astype(BF16),
        "ab": w_in0[:, ab0:ab0 + 2 * GDN_HEADS],
        "gdn_o": w_gdn_o[0].astype(BF16), "sc_o": w_sc_o[0].astype(BF16), "mix": w_mix_out[0].astype(BF16),
        "xq": w_xq[0].astype(BF16), "xkv": w_xkv[0].astype(BF16), "xo": w_xo[0].astype(BF16),
        "up": w_mlp_up[0].astype(BF16), "down": w_mlp_down[0].astype(BF16),
        "norm_mlp": norm_mlp_w[0], "norm_f": norm_f_w,
    }
    tm, tn = 512, 1024

    rows_p = batch * seq
    xp = x_prompt.reshape(rows_p, d)
    memn = rmsnorm_rows(mem_prompt.reshape(batch * N_MEM, d), mem_norm_w[0], tm)
    mkv = matmul_rows(memn, w["xkv"], F32, tm, tn)
    xn, gb = rmsnorm_gates_rows(xp, norm_mix_w[0], w["ab"], gdn_A_log[0], gdn_dt_bias[0], tm)
    proj = matmul_rows(xn, w["main"], BF16, tm, tn)
    q, k, v, u_sc, cx_tail = prep_prompt(proj, gdn_conv_w[0], sc_conv_w[0], batch, seq, 256)
    o_gdn, s_prompt = gdn_prompt(q, k, v, proj, gb, gdn_out_norm_w[0], batch, seq)
    x1, xn1 = merge_mixers(o_gdn, u_sc, proj, xp, w["gdn_o"], w["sc_o"], w["mix"], norm_x_w[0], tm)
    mkv3 = mkv.reshape(batch, N_MEM, 2 * d)
    y_prompt = _layer_tail(x1, xn1, lambda hq: attention_prompt(hq, mkv3, batch, seq, tm), w, tm, tn)
    p_conv = proj.reshape(batch, seq, MAIN_COLS)[:, seq - (GDN_CONV_W - 1):, :3 * d].astype(F32)
    p_sc = cx_tail[:, SUBLANES - (SC_CONV_W - 1):]
    p_mk = mkv3[:, :, :d].reshape(batch, N_MEM, X_HEADS, X_HEAD_DIM)
    p_mv = mkv3[:, :, d:].reshape(batch, N_MEM, X_HEADS, X_HEAD_DIM)

    rows_s = n_seq * t_dec
    xs = x_sample.reshape(rows_s, d)
    xn_s, gb_s = rmsnorm_gates_rows(xs, norm_mix_w[0], w["ab"], gdn_A_log[0], gdn_dt_bias[0], tm)
    proj_s = matmul_rows(xn_s, w["main"], F32, tm, tn)
    qst8 = jnp.pad(state_gdn_conv[0], ((0, 0), (0, SUBLANES - (GDN_CONV_W - 1)), (0, 0))).reshape(rows_s, 3 * d)
    cst8 = jnp.pad(state_sc_conv[0], ((0, 0), (0, SUBLANES - (SC_CONV_W - 1)), (0, 0))).reshape(rows_s, d)
    q_s, k_s, v_s, u_s, cx_s = prep_sample(proj_s, qst8, cst8, gdn_conv_w[0], sc_conv_w[0], n_seq, 16)
    o_s, s_sample = gdn_sample(q_s, k_s, v_s, proj_s, gb_s, gdn_out_norm_w[0], state_gdn[0], n_seq, 4)
    x1_s, xn1_s = merge_mixers(o_s, u_s, proj_s, xs, w["gdn_o"], w["sc_o"], w["mix"], norm_x_w[0], tm)
    y_sample = _layer_tail(
        x1_s, xn1_s, lambda hq: attention_sample(hq.astype(F32), cache_mem_k[0], cache_mem_v[0], n_seq, 4), w, tm, tn)
    s_conv = proj_s.reshape(n_seq, t_dec, MAIN_COLS)[:, t_dec - (GDN_CONV_W - 1):, :3 * d]
    s_sc = cx_s.reshape(n_seq, t_dec, d)[:, t_dec - (SC_CONV_W - 1):]

    return (y_prompt.reshape(batch, seq, d), y_sample.reshape(n_seq, t_dec, d),
            p_mk[None], p_mv[None], p_conv[None], s_prompt[None], p_sc[None],
            s_conv[None], s_sample[None], s_sc[None])
```

```python
import functools

import jax
import jax.numpy as jnp
from jax import lax
from jax.experimental import pallas as pl
from jax.experimental.pallas import tpu as pltpu

F32 = jnp.float32
BF16 = jnp.bfloat16

D_MODEL = 1024
GDN_HEADS = 8
HEAD_DIM = 128
GDN_CHUNK = 64
GDN_CONV_W = 4
SC_CONV_W = 3
X_HEADS = 4
X_HEAD_DIM = 256
N_MEM = 256
D_FF = 4096
EPS = 1e-6
SUBLANES = 8
VMEM_LIMIT = 48 * 1024 * 1024

COL_Z = 3 * D_MODEL
COL_BG = 4 * D_MODEL
COL_CG = 5 * D_MODEL
COL_XIN = 6 * D_MODEL
COL_GA = 7 * D_MODEL
COL_GB = 8 * D_MODEL
MAIN_COLS = 9 * D_MODEL


def _params(n_axes):
    return pltpu.CompilerParams(dimension_semantics=("arbitrary",) * n_axes, vmem_limit_bytes=VMEM_LIMIT)


def _dot(a, b):
    return jnp.dot(a.astype(BF16), b.astype(BF16), preferred_element_type=F32)


def _dot_nt(a, b):
    return lax.dot_general(a.astype(BF16), b.astype(BF16), (((1,), (1,)), ((), ())), preferred_element_type=F32)


def _dot_tn(a, b):
    return lax.dot_general(a.astype(BF16), b.astype(BF16), (((0,), (0,)), ((), ())), preferred_element_type=F32)


def _split_hi_lo(x):
    hi = x.astype(BF16)
    lo = (x - hi.astype(F32)).astype(BF16)
    return hi, lo


def _dot_exact_lhs(m01, x):
    hi, lo = _split_hi_lo(x)
    m = m01.astype(BF16)
    return jnp.dot(m, hi, preferred_element_type=F32) + jnp.dot(m, lo, preferred_element_type=F32)


def _rms(x, w):
    return x * lax.rsqrt(jnp.mean(x * x, axis=-1, keepdims=True) + EPS) * w


def _sigmoid(x):
    return 1.0 / (1.0 + jnp.exp(-x))


def _silu(x):
    return x * _sigmoid(x)


def _softplus(x):
    return jnp.maximum(x, 0.0) + jnp.log(1.0 + jnp.exp(-jnp.abs(x)))


def _norm_kernel(x_ref, w_ref, o_ref):
    o_ref[...] = _rms(x_ref[...], w_ref[...]).astype(o_ref.dtype)


def rmsnorm_rows(x, w, tm):
    rows, d = x.shape
    return pl.pallas_call(
        _norm_kernel,
        grid=(rows // tm,),
        in_specs=[pl.BlockSpec((tm, d), lambda i: (i, 0)), pl.BlockSpec((1, d), lambda i: (0, 0))],
        out_specs=pl.BlockSpec((tm, d), lambda i: (i, 0)),
        out_shape=jax.ShapeDtypeStruct((rows, d), BF16),
        compiler_params=_params(1),
        name="rmsnorm_rows",
    )(x, w.reshape(1, d))


def _norm_gates_kernel(x_ref, w_ref, wab_ref, alog_ref, dtb_ref, o_ref, gb_ref):
    xn = _rms(x_ref[...], w_ref[...]).astype(BF16)
    o_ref[...] = xn
    ab = jnp.dot(xn, wab_ref[...], preferred_element_type=F32)
    a = ab[:, :GDN_HEADS]
    b = ab[:, GDN_HEADS:2 * GDN_HEADS]
    g = -jnp.exp(alog_ref[...]) * _softplus(a + dtb_ref[...])
    gb_ref[...] = jnp.concatenate([g, _sigmoid(b)], axis=1)


def rmsnorm_gates_rows(x, w, w_ab, a_log, dt_bias, tm):
    rows, d = x.shape
    wab = jnp.zeros((d, 128), BF16).at[:, :2 * GDN_HEADS].set(w_ab.astype(BF16))
    return pl.pallas_call(
        _norm_gates_kernel,
        grid=(rows // tm,),
        in_specs=[
            pl.BlockSpec((tm, d), lambda i: (i, 0)),
            pl.BlockSpec((1, d), lambda i: (0, 0)),
            pl.BlockSpec((d, 128), lambda i: (0, 0)),
            pl.BlockSpec((1, GDN_HEADS), lambda i: (0, 0)),
            pl.BlockSpec((1, GDN_HEADS), lambda i: (0, 0)),
        ],
        out_specs=[pl.BlockSpec((tm, d), lambda i: (i, 0)), pl.BlockSpec((tm, 2 * GDN_HEADS), lambda i: (i, 0))],
        out_shape=[jax.ShapeDtypeStruct((rows, d), BF16), jax.ShapeDtypeStruct((rows, 2 * GDN_HEADS), F32)],
        compiler_params=_params(1),
        name="rmsnorm_gates_rows",
    )(x, w.reshape(1, d), wab, a_log.reshape(1, GDN_HEADS), dt_bias.reshape(1, GDN_HEADS))


def _mm_kernel(a_ref, w_ref, o_ref, *, relu2):
    acc = _dot(a_ref[...], w_ref[...])
    if relu2:
        acc = jnp.square(jnp.maximum(acc, 0.0))
    o_ref[...] = acc.astype(o_ref.dtype)


def matmul_rows(a, w, out_dtype, tm, tn, relu2=False):
    rows, k = a.shape
    n = w.shape[1]
    return pl.pallas_call(
        functools.partial(_mm_kernel, relu2=relu2),
        grid=(n // tn, rows // tm),
        in_specs=[pl.BlockSpec((tm, k), lambda j, i: (i, 0)), pl.BlockSpec((k, tn), lambda j, i: (0, j))],
        out_specs=pl.BlockSpec((tm, tn), lambda j, i: (i, j)),
        out_shape=jax.ShapeDtypeStruct((rows, n), out_dtype),
        compiler_params=_params(2),
        name="matmul_rows",
    )(a, w)


def _mm_res_norm_kernel(a_ref, w_ref, res_ref, nw_ref, x_ref, xn_ref):
    x = res_ref[...] + _dot(a_ref[...], w_ref[...])
    x_ref[...] = x
    xn_ref[...] = _rms(x, nw_ref[...]).astype(xn_ref.dtype)


def matmul_residual_norm(a, w, res, norm_w, xn_dtype, tm):
    rows, k = a.shape
    n = w.shape[1]
    return pl.pallas_call(
        _mm_res_norm_kernel,
        grid=(rows // tm,),
        in_specs=[
            pl.BlockSpec((tm, k), lambda i: (i, 0)),
            pl.BlockSpec((k, n), lambda i: (0, 0)),
            pl.BlockSpec((tm, n), lambda i: (i, 0)),
            pl.BlockSpec((1, n), lambda i: (0, 0)),
        ],
        out_specs=[pl.BlockSpec((tm, n), lambda i: (i, 0)), pl.BlockSpec((tm, n), lambda i: (i, 0))],
        out_shape=[jax.ShapeDtypeStruct((rows, n), F32), jax.ShapeDtypeStruct((rows, n), xn_dtype)],
        compiler_params=_params(1),
        name="matmul_residual_norm",
    )(a, w, res, norm_w.reshape(1, n))


def _merge_kernel(oa_ref, ub_ref, ga_ref, gb_ref, x_ref, wa_ref, wb_ref, wm_ref, nw_ref, x1_ref, xn_ref):
    ya = _dot(oa_ref[...], wa_ref[...])
    yb = _dot(ub_ref[...], wb_ref[...])
    merged = _sigmoid(ga_ref[...].astype(F32)) * ya + _sigmoid(gb_ref[...].astype(F32)) * yb
    x1 = x_ref[...] + _dot(merged, wm_ref[...])
    x1_ref[...] = x1
    xn_ref[...] = _rms(x1, nw_ref[...]).astype(xn_ref.dtype)


def merge_mixers(o_a, u_b, proj, x, w_gdn_o, w_sc_o, w_mix_out, norm_w, tm):
    rows, d = x.shape
    row_blk = lambda i: (i, 0)
    full = lambda i: (0, 0)
    return pl.pallas_call(
        _merge_kernel,
        grid=(rows // tm,),
        in_specs=[
            pl.BlockSpec((tm, d), row_blk),
            pl.BlockSpec((tm, d), row_blk),
            pl.BlockSpec((tm, d), lambda i: (i, COL_GA // D_MODEL)),
            pl.BlockSpec((tm, d), lambda i: (i, COL_GB // D_MODEL)),
            pl.BlockSpec((tm, d), row_blk),
            pl.BlockSpec((d, d), full),
            pl.BlockSpec((d, d), full),
            pl.BlockSpec((d, d), full),
            pl.BlockSpec((1, d), full),
        ],
        out_specs=[pl.BlockSpec((tm, d), row_blk), pl.BlockSpec((tm, d), row_blk)],
        out_shape=[jax.ShapeDtypeStruct((rows, d), F32), jax.ShapeDtypeStruct((rows, d), BF16)],
        compiler_params=_params(1),
        name="merge_mixers",
    )(o_a, u_b, proj, proj, x, w_gdn_o, w_sc_o, w_mix_out, norm_w.reshape(1, d))


def _mm_res_final_kernel(a_ref, w_ref, res_ref, nw_ref, y_ref):
    x = res_ref[...] + _dot(a_ref[...], w_ref[...])
    y_ref[...] = _rms(x, nw_ref[...])


def matmul_residual_final_norm(a, w, res, norm_w, tm):
    rows, k = a.shape
    n = w.shape[1]
    return pl.pallas_call(
        _mm_res_final_kernel,
        grid=(rows // tm,),
        in_specs=[
            pl.BlockSpec((tm, k), lambda i: (i, 0)),
            pl.BlockSpec((k, n), lambda i: (0, 0)),
            pl.BlockSpec((tm, n), lambda i: (i, 0)),
            pl.BlockSpec((1, n), lambda i: (0, 0)),
        ],
        out_specs=pl.BlockSpec((tm, n), lambda i: (i, 0)),
        out_shape=jax.ShapeDtypeStruct((rows, n), F32),
        compiler_params=_params(1),
        name="matmul_residual_final_norm",
    )(a, w, res, norm_w.reshape(1, n))


def _shifted(x, prev_tail, s):
    rolled = pltpu.roll(x, s, 0)
    head = pltpu.roll(prev_tail, s, 0)
    row = lax.broadcasted_iota(jnp.int32, head.shape, 0)
    first = jnp.where(row < s, head, rolled[:SUBLANES])
    if x.shape[0] == SUBLANES:
        return first
    return jnp.concatenate([first, rolled[SUBLANES:]], axis=0)


def _shifted_groups(x, hist, s, n_hist):
    rows = x.shape[0]
    rolled = pltpu.roll(x, s, 0)
    head = hist if n_hist == s else pltpu.roll(hist, rows - (n_hist - s), 0)
    row = lax.broadcasted_iota(jnp.int32, x.shape, 0)
    return jnp.where((row & (SUBLANES - 1)) < s, head, rolled)


def _causal_conv(x, w, shift_fn):
    width = w.shape[0]
    y = x * w[width - 1:width]
    for i in range(width - 1):
        y = y + shift_fn(width - 1 - i) * w[i:i + 1]
    return y


def _l2norm_heads(x, scale):
    outs = []
    for h in range(GDN_HEADS):
        xh = x[:, h * HEAD_DIM:(h + 1) * HEAD_DIM]
        outs.append(xh * (lax.rsqrt(jnp.sum(xh * xh, axis=-1, keepdims=True) + EPS) * scale))
    return jnp.concatenate(outs, axis=1)


def _prep_body(qkv, bg, cg, xin, cw, sw, qkv_shift, cx_shift):
    conv = _silu(_causal_conv(qkv, cw, qkv_shift))
    q = _l2norm_heads(conv[:, :D_MODEL], HEAD_DIM ** -0.5)
    k = _l2norm_heads(conv[:, D_MODEL:2 * D_MODEL], 1.0)
    v = conv[:, 2 * D_MODEL:]
    cx = cg * xin
    u = bg * _causal_conv(cx, sw, functools.partial(cx_shift, cx))
    return q, k, v, u, cx


def _prep_prompt_kernel(qkv_ref, bg_ref, cg_ref, xin_ref, cw_ref, sw_ref,
                        q_ref, k_ref, v_ref, u_ref, cxt_ref, qkv_tail, cx_tail):
    @pl.when(pl.program_id(1) == 0)
    def _():
        qkv_tail[...] = jnp.zeros_like(qkv_tail)
        cx_tail[...] = jnp.zeros_like(cx_tail)

    qkv = qkv_ref[...].astype(F32)
    q, k, v, u, cx = _prep_body(qkv, bg_ref[...].astype(F32), cg_ref[...].astype(F32), xin_ref[...].astype(F32),
                                cw_ref[...], sw_ref[...],
                                lambda s: _shifted(qkv, qkv_tail[...], s),
                                lambda cx, s: _shifted(cx, cx_tail[...], s))
    q_ref[...] = q.astype(q_ref.dtype)
    k_ref[...] = k.astype(k_ref.dtype)
    v_ref[...] = v.astype(v_ref.dtype)
    u_ref[...] = u.astype(u_ref.dtype)
    tm = qkv.shape[0]
    qkv_tail[...] = qkv[tm - SUBLANES:]
    cx_tail[...] = cx[tm - SUBLANES:]
    cxt_ref[0] = cx[tm - SUBLANES:]


def prep_prompt(proj, conv_w, sc_w, batch, seq, tm):
    rows = batch * seq
    nt = seq // tm
    d = D_MODEL
    blk = lambda c: pl.BlockSpec((tm, d), lambda b, t, c=c: (b * nt + t, c))
    out_rows = pl.BlockSpec((tm, d), lambda b, t: (b * nt + t, 0))
    return pl.pallas_call(
        _prep_prompt_kernel,
        grid=(batch, nt),
        in_specs=[
            pl.BlockSpec((tm, 3 * d), lambda b, t: (b * nt + t, 0)),
            blk(COL_BG // d), blk(COL_CG // d), blk(COL_XIN // d),
            pl.BlockSpec((GDN_CONV_W, 3 * d), lambda b, t: (0, 0)),
            pl.BlockSpec((SC_CONV_W, d), lambda b, t: (0, 0)),
        ],
        out_specs=[out_rows, out_rows, out_rows, out_rows, pl.BlockSpec((1, SUBLANES, d), lambda b, t: (b, 0, 0))],
        out_shape=[jax.ShapeDtypeStruct((rows, d), BF16)] * 4 + [jax.ShapeDtypeStruct((batch, SUBLANES, d), F32)],
        scratch_shapes=[pltpu.VMEM((SUBLANES, 3 * d), F32), pltpu.VMEM((SUBLANES, d), F32)],
        compiler_params=_params(2),
        name="prep_prompt",
    )(proj, proj, proj, proj, conv_w, sc_w)


def _prep_sample_kernel(qkv_ref, bg_ref, cg_ref, xin_ref, qst_ref, cst_ref, cw_ref, sw_ref,
                        q_ref, k_ref, v_ref, u_ref, cx_ref):
    qkv = qkv_ref[...]
    q, k, v, u, cx = _prep_body(qkv, bg_ref[...], cg_ref[...], xin_ref[...], cw_ref[...], sw_ref[...],
                                lambda s: _shifted_groups(qkv, qst_ref[...], s, GDN_CONV_W - 1),
                                lambda cx, s: _shifted_groups(cx, cst_ref[...], s, SC_CONV_W - 1))
    q_ref[...] = q
    k_ref[...] = k
    v_ref[...] = v
    u_ref[...] = u
    cx_ref[...] = cx


def prep_sample(proj, qkv_state8, cx_state8, conv_w, sc_w, n_seq, per_step):
    rows = n_seq * SUBLANES
    d = D_MODEL
    t = SUBLANES * per_step
    blk = lambda c: pl.BlockSpec((t, d), lambda n, c=c: (n, c))
    out_rows = pl.BlockSpec((t, d), lambda n: (n, 0))
    return pl.pallas_call(
        _prep_sample_kernel,
        grid=(n_seq // per_step,),
        in_specs=[
            pl.BlockSpec((t, 3 * d), lambda n: (n, 0)),
            blk(COL_BG // d), blk(COL_CG // d), blk(COL_XIN // d),
            pl.BlockSpec((t, 3 * d), lambda n: (n, 0)),
            pl.BlockSpec((t, d), lambda n: (n, 0)),
            pl.BlockSpec((GDN_CONV_W, 3 * d), lambda n: (0, 0)),
            pl.BlockSpec((SC_CONV_W, d), lambda n: (0, 0)),
        ],
        out_specs=[out_rows] * 5,
        out_shape=[jax.ShapeDtypeStruct((rows, d), F32)] * 5,
        compiler_params=_params(1),
        name="prep_sample",
    )(proj, proj, proj, proj, qkv_state8, cx_state8, conv_w, sc_w)


def _tri_inverse(l_mats, n, top, tick):
    ri = lax.broadcasted_iota(jnp.int32, (n, n), 0)
    ci = lax.broadcasted_iota(jnp.int32, (n, n), 1)
    eye = jnp.where(ri == ci, 1.0, 0.0)
    ts = [eye - jnp.where((ri ^ ci) == 1, l, 0.0) for l in l_mats]
    shift = 1
    while (1 << shift) < top:
        pair = ((ri >> shift) ^ (ci >> shift)) == 1
        xs = [_dot(jnp.where(pair, l, 0.0), t) for l, t in zip(l_mats, ts)]
        tick()
        ts = [t - _dot(t, x) for t, x in zip(ts, xs)]
        tick()
        shift += 1
    return ts


def _pair_diff(col_b):
    hi, lo = _split_hi_lo(col_b)
    hi = hi.astype(F32)
    lo = lo.astype(F32)
    lane = lax.broadcasted_iota(jnp.int32, col_b.shape, 1)
    left = jnp.where(lane == 0, hi, jnp.where(lane == 1, lo, jnp.where(lane < 4, 1.0, 0.0)))
    right = jnp.where(lane == 2, -hi, jnp.where(lane == 3, -lo, jnp.where(lane < 2, 1.0, 0.0)))
    return _dot_nt(left, right)


def _gdn_intra(qs, ks, vs, gcum_bs, beta_bs, block, tick=lambda: None):
    n = qs[0].shape[0]
    ri = lax.broadcasted_iota(jnp.int32, (n, n), 0)
    ci = lax.broadcasted_iota(jnp.int32, (n, n), 1)
    same = (ri >> (block.bit_length() - 1)) == (ci >> (block.bit_length() - 1))
    diffs = [_pair_diff(g) for g in gcum_bs]
    tick()
    decays = [jnp.where(same, jnp.where(ci <= ri, jnp.exp(jnp.minimum(d, 0.0)), 0.0), 0.0) for d in diffs]
    kks = [_dot_nt(k, k) for k in ks]
    tick()
    l_mats = [b[:, :n] * kk * jnp.where(ci < ri, dec, 0.0) for b, kk, dec in zip(beta_bs, kks, decays)]
    ts = _tri_inverse(l_mats, n, block, tick)
    egcs = [jnp.exp(g) for g in gcum_bs]
    wus = [_dot(t, jnp.concatenate([k * (b * e), v * b], axis=1))
           for t, k, v, b, e in zip(ts, ks, vs, beta_bs, egcs)]
    tick()
    ps = [_dot_nt(q, k) * dec for q, k, dec in zip(qs, ks, decays)]
    tick()
    return ([wu[:, :HEAD_DIM] for wu in wus], [wu[:, HEAD_DIM:] for wu in wus],
            [q * e for q, e in zip(qs, egcs)], ps)


def _gated_out(o, z, onw):
    return _rms(o, onw) * _silu(z)


def _gdn_prompt_kernel(q_ref, k_ref, v_ref, gb_ref, z_ref, onw_ref, o_ref, s_out_ref,
                       s_ref, wqe_scr, p_scr, ke_scr, uv_scr, eg_scr, *, n_chunks, blocks_per_seq):
    g = pl.program_id(0)
    n = GDN_CHUNK
    heads = range(GDN_HEADS)
    chunks = range(n_chunks)
    sls = [slice(h * HEAD_DIM, (h + 1) * HEAD_DIM) for h in heads]

    @pl.when(g == 0)
    def _():
        s_ref[...] = jnp.zeros_like(s_ref)
        wqe_scr[...] = jnp.zeros_like(wqe_scr)
        p_scr[...] = jnp.zeros_like(p_scr)
        ke_scr[...] = jnp.zeros_like(ke_scr)
        uv_scr[...] = jnp.zeros_like(uv_scr)
        eg_scr[...] = jnp.zeros_like(eg_scr)

    wslot = lax.rem(g, 2)
    rslot = 1 - wslot
    first_of_seq = lax.rem(g + blocks_per_seq - 1, blocks_per_seq) == 0
    state = {"s": [jnp.where(first_of_seq, 0.0, s_ref[h]) for h in heads]}

    def chain_a(c):
        state["wq"] = [_dot(wqe_scr[rslot, c, h], state["s"][h]) for h in heads]

    def chain_b(c):
        rows = slice(c * n, (c + 1) * n)
        us = [uv_scr[rslot, c, h] - state["wq"][h][:n] for h in heads]
        pus = [_dot(p_scr[rslot, c, h], us[h]) for h in heads]
        kus = [_dot_tn(ke_scr[rslot, c, h], us[h]) for h in heads]
        state["s"] = [eg_scr[rslot, c, h][0:1, :] * state["s"][h] + kus[h] for h in heads]
        for h in heads:
            o = state["wq"][h][n:] + pus[h]
            o_ref[rows, sls[h]] = _gated_out(o, z_ref[rows, sls[h]].astype(F32), onw_ref[...]).astype(o_ref.dtype)

    pending = [functools.partial(f, c) for c in chunks for f in (chain_a, chain_b)]

    def tick():
        if pending:
            pending.pop(0)()

    ri = lax.broadcasted_iota(jnp.int32, (n, n), 0)
    ci = lax.broadcasted_iota(jnp.int32, (n, n), 1)
    tri = jnp.where(ci <= ri, 1.0, 0.0)
    probs = [(c, h) for c in chunks for h in heads]
    gbs = [gb_ref[c * n:(c + 1) * n, :] for c in chunks]
    gcums = [_dot_exact_lhs(tri, gb) for gb in gbs]
    tick()
    qs = [q_ref[c * n:(c + 1) * n, sls[h]].astype(F32) for c, h in probs]
    ks = [k_ref[c * n:(c + 1) * n, sls[h]].astype(F32) for c, h in probs]
    vs = [v_ref[c * n:(c + 1) * n, sls[h]].astype(F32) for c, h in probs]
    gcum_bs = [jnp.broadcast_to(gcums[c][:, h:h + 1], (n, HEAD_DIM)) for c, h in probs]
    beta_bs = [jnp.broadcast_to(gbs[c][:, GDN_HEADS + h:GDN_HEADS + h + 1], (n, HEAD_DIM)) for c, h in probs]
    ws, uvs, qes, ps = _gdn_intra(qs, ks, vs, gcum_bs, beta_bs, n, tick)
    for i, (c, h) in enumerate(probs):
        glast = gcum_bs[i][n - 1:n, :]
        wqe_scr[wslot, c, h] = jnp.concatenate([ws[i], qes[i]], axis=0).astype(BF16)
        uv_scr[wslot, c, h] = uvs[i]
        p_scr[wslot, c, h] = ps[i].astype(BF16)
        ke_scr[wslot, c, h] = (ks[i] * jnp.exp(glast - gcum_bs[i])).astype(BF16)
        eg_scr[wslot, c, h] = jnp.broadcast_to(jnp.exp(glast), (SUBLANES, HEAD_DIM))
    while pending:
        tick()
    for h in heads:
        s_ref[h] = state["s"][h]
        s_out_ref[0, h] = state["s"][h]


def gdn_prompt(q, k, v, proj, gb, out_norm_w, batch, seq, n_chunks):
    rows = batch * seq
    n = GDN_CHUNK
    tm = n * n_chunks
    blocks_per_seq = seq // tm
    n_blocks = rows // tm
    d = D_MODEL
    cur = lambda g: (jnp.minimum(g, n_blocks - 1), 0)
    prev = lambda g: (jnp.maximum(g - 1, 0), 0)
    per_head = lambda lead, dtype: pltpu.VMEM((2, n_chunks, GDN_HEADS) + lead, dtype)
    return pl.pallas_call(
        functools.partial(_gdn_prompt_kernel, n_chunks=n_chunks, blocks_per_seq=blocks_per_seq),
        grid=(n_blocks + 1,),
        in_specs=[
            pl.BlockSpec((tm, d), cur), pl.BlockSpec((tm, d), cur), pl.BlockSpec((tm, d), cur),
            pl.BlockSpec((tm, 2 * GDN_HEADS), cur),
            pl.BlockSpec((tm, d), lambda g: (jnp.maximum(g - 1, 0), COL_Z // d)),
            pl.BlockSpec((1, HEAD_DIM), lambda g: (0, 0)),
        ],
        out_specs=[pl.BlockSpec((tm, d), prev),
                   pl.BlockSpec((1, GDN_HEADS, HEAD_DIM, HEAD_DIM),
                                lambda g: (jnp.maximum(g - 1, 0) // blocks_per_seq, 0, 0, 0))],
        out_shape=[jax.ShapeDtypeStruct((rows, d), BF16),
                   jax.ShapeDtypeStruct((batch, GDN_HEADS, HEAD_DIM, HEAD_DIM), F32)],
        scratch_shapes=[
            pltpu.VMEM((GDN_HEADS, HEAD_DIM, HEAD_DIM), F32),
            per_head((2 * n, HEAD_DIM), BF16),
            per_head((n, n), BF16),
            per_head((n, HEAD_DIM), BF16),
            per_head((n, HEAD_DIM), F32),
            per_head((SUBLANES, HEAD_DIM), F32),
        ],
        compiler_params=_params(1),
        name="gdn_prompt",
    )(q, k, v, gb, proj, out_norm_w.reshape(1, HEAD_DIM))


def _stack_heads(x):
    return jnp.concatenate([x[:, h * HEAD_DIM:(h + 1) * HEAD_DIM] for h in range(GDN_HEADS)], axis=0)


def _gdn_sample_kernel(q_ref, k_ref, v_ref, z_ref, gb_ref, onw_ref, s_in_ref, o_ref, s_out_ref, *, n_seq):
    t = SUBLANES
    n = GDN_HEADS * t
    seqs = range(n_seq)
    heads = range(GDN_HEADS)
    rows = [slice(i * t, (i + 1) * t) for i in seqs]
    qs = [_stack_heads(q_ref[r, :]) for r in rows]
    ks = [_stack_heads(k_ref[r, :]) for r in rows]
    vs = [_stack_heads(v_ref[r, :]) for r in rows]
    gbs = [gb_ref[r, :] for r in rows]
    g_bs = [jnp.concatenate([jnp.broadcast_to(gb[:, h:h + 1], (t, HEAD_DIM)) for h in heads], axis=0) for gb in gbs]
    beta_bs = [jnp.concatenate([jnp.broadcast_to(gb[:, GDN_HEADS + h:GDN_HEADS + h + 1], (t, HEAD_DIM))
                                for h in heads], axis=0) for gb in gbs]
    ri = lax.broadcasted_iota(jnp.int32, (n, n), 0)
    ci = lax.broadcasted_iota(jnp.int32, (n, n), 1)
    cum_mask = jnp.where(((ri >> 3) == (ci >> 3)), jnp.where(ci <= ri, 1.0, 0.0), 0.0)
    gcum_bs = [_dot_exact_lhs(cum_mask, g) for g in g_bs]
    ws, uvs, qes, ps = _gdn_intra(qs, ks, vs, gcum_bs, beta_bs, t)
    glast_bs = [jnp.concatenate([jnp.broadcast_to(g[h * t + t - 1:h * t + t, :], (t, HEAD_DIM)) for h in heads],
                                axis=0) for g in gcum_bs]
    kes = [k * jnp.exp(gl - g) for k, gl, g in zip(ks, glast_bs, gcum_bs)]
    wqes = [jnp.concatenate([w, qe], axis=0) for w, qe in zip(ws, qes)]
    wqs = [[_dot(wqes[i], s_in_ref[i, h]) for h in heads] for i in seqs]
    us = [uvs[i] - jnp.concatenate([wqs[i][h][h * t:(h + 1) * t] for h in heads], axis=0) for i in seqs]
    os_ = [jnp.concatenate([wqs[i][h][n + h * t:n + (h + 1) * t] for h in heads], axis=0) + _dot(ps[i], us[i])
           for i in seqs]
    row = lax.broadcasted_iota(jnp.int32, (n, HEAD_DIM), 0)
    for i in seqs:
        for h in heads:
            u_h = jnp.where((row >> 3) == h, us[i], 0.0)
            s_out_ref[i, h] = jnp.exp(glast_bs[i][h * t:h * t + 1, :]) * s_in_ref[i, h] + _dot_tn(kes[i], u_h)
    for i in seqs:
        og = _gated_out(os_[i], _stack_heads(z_ref[rows[i], :]), onw_ref[...])
        for h in heads:
            o_ref[rows[i], h * HEAD_DIM:(h + 1) * HEAD_DIM] = og[h * t:(h + 1) * t]


def gdn_sample(q, k, v, proj, gb, out_norm_w, state, n_seq, per_step):
    t = SUBLANES
    d = D_MODEL
    rowblk = pl.BlockSpec((per_step * t, d), lambda i: (i, 0))
    sblk = pl.BlockSpec((per_step, GDN_HEADS, HEAD_DIM, HEAD_DIM), lambda i: (i, 0, 0, 0))
    return pl.pallas_call(
        functools.partial(_gdn_sample_kernel, n_seq=per_step),
        grid=(n_seq // per_step,),
        in_specs=[
            rowblk, rowblk, rowblk,
            pl.BlockSpec((per_step * t, d), lambda i: (i, COL_Z // d)),
            pl.BlockSpec((per_step * t, 2 * GDN_HEADS), lambda i: (i, 0)),
            pl.BlockSpec((1, HEAD_DIM), lambda i: (0, 0)),
            sblk,
        ],
        out_specs=[rowblk, sblk],
        out_shape=[jax.ShapeDtypeStruct((n_seq * t, d), F32),
                   jax.ShapeDtypeStruct((n_seq, GDN_HEADS, HEAD_DIM, HEAD_DIM), F32)],
        compiler_params=_params(1),
        name="gdn_sample",
    )(q, k, v, proj, gb, out_norm_w.reshape(1, HEAD_DIM), state)


def _attend(q, load_k, load_v):
    outs = []
    for h in range(X_HEADS):
        s = _dot_nt(q[:, h * X_HEAD_DIM:(h + 1) * X_HEAD_DIM], load_k(h)) * (X_HEAD_DIM ** -0.5)
        e = jnp.exp(s - jnp.max(s, axis=-1, keepdims=True))
        p = e / jnp.sum(e, axis=-1, keepdims=True)
        outs.append(_dot(p, load_v(h)))
    return jnp.concatenate(outs, axis=1)


def _attn_prompt_kernel(q_ref, mk_ref, mv_ref, o_ref):
    head = lambda ref: lambda h: ref[0, :, h * X_HEAD_DIM:(h + 1) * X_HEAD_DIM]
    o_ref[...] = _attend(q_ref[...], head(mk_ref), head(mv_ref)).astype(o_ref.dtype)


def attention_prompt(q, mkv, batch, seq, tm):
    nt = seq // tm
    d = D_MODEL
    return pl.pallas_call(
        _attn_prompt_kernel,
        grid=(batch, nt),
        in_specs=[
            pl.BlockSpec((tm, d), lambda b, t: (b * nt + t, 0)),
            pl.BlockSpec((1, N_MEM, d), lambda b, t: (b, 0, 0)),
            pl.BlockSpec((1, N_MEM, d), lambda b, t: (b, 0, 1)),
        ],
        out_specs=pl.BlockSpec((tm, d), lambda b, t: (b * nt + t, 0)),
        out_shape=jax.ShapeDtypeStruct((batch * seq, d), BF16),
        compiler_params=_params(2),
        name="attention_prompt",
    )(q, mkv, mkv)


def _attn_sample_kernel(q_ref, mk_ref, mv_ref, o_ref, *, n_seq):
    t = SUBLANES
    n_rows = N_MEM * X_HEADS
    seqs = range(n_seq)
    heads = range(X_HEADS)
    scores = []
    for i in seqs:
        q = q_ref[i * t:(i + 1) * t, :]
        q_all = jnp.concatenate([q[:, h * X_HEAD_DIM:(h + 1) * X_HEAD_DIM] for h in heads], axis=0)
        scores.append(_dot_nt(q_all, mk_ref[i].reshape(n_rows, X_HEAD_DIM)) * (X_HEAD_DIM ** -0.5))
    row_head = lax.broadcasted_iota(jnp.int32, (X_HEADS * t, n_rows), 0) >> 3
    col_head = lax.broadcasted_iota(jnp.int32, (X_HEADS * t, n_rows), 1) & (X_HEADS - 1)
    same_head = row_head == col_head
    probs = []
    for s in scores:
        s = jnp.where(same_head, s, -jnp.inf)
        e = jnp.exp(s - jnp.max(s, axis=-1, keepdims=True))
        probs.append(e / jnp.sum(e, axis=-1, keepdims=True))
    ctxs = [_dot(p, mv_ref[i].reshape(n_rows, X_HEAD_DIM)) for p, i in zip(probs, seqs)]
    for c, i in zip(ctxs, seqs):
        for h in heads:
            o_ref[i * t:(i + 1) * t, h * X_HEAD_DIM:(h + 1) * X_HEAD_DIM] = c[h * t:(h + 1) * t]


def attention_sample(q, mem_k, mem_v, n_seq, per_step):
    t = SUBLANES
    d = D_MODEL
    return pl.pallas_call(
        functools.partial(_attn_sample_kernel, n_seq=per_step),
        grid=(n_seq // per_step,),
        in_specs=[
            pl.BlockSpec((per_step * t, d), lambda i: (i, 0)),
            pl.BlockSpec((per_step, N_MEM, X_HEADS, X_HEAD_DIM), lambda i: (i, 0, 0, 0)),
            pl.BlockSpec((per_step, N_MEM, X_HEADS, X_HEAD_DIM), lambda i: (i, 0, 0, 0)),
        ],
        out_specs=pl.BlockSpec((per_step * t, d), lambda i: (i, 0)),
        out_shape=jax.ShapeDtypeStruct((n_seq * t, d), F32),
        compiler_params=_params(1),
        name="attention_sample",
    )(q, mem_k, mem_v)


def _layer_tail(x1, xn1, ctx_fn, w, tm, tn):
    hq = matmul_rows(xn1, w["xq"], xn1.dtype, tm, tn)
    ctx = ctx_fn(hq)
    x2, xn2 = matmul_residual_norm(ctx, w["xo"], x1, w["norm_mlp"], BF16, tm)
    hm = matmul_rows(xn2, w["up"], BF16, tm, tn, relu2=True)
    return matmul_residual_final_norm(hm, w["down"], x2, w["norm_f"], tm)


def kernel(x_prompt, x_sample, mem_prompt, cache_mem_k, cache_mem_v, state_gdn_conv, state_gdn, state_sc_conv, norm_mix_w, w_in, gdn_conv_w, gdn_A_log, gdn_dt_bias, gdn_out_norm_w, w_gdn_o, sc_conv_w, w_sc_o, w_mix_out, norm_x_w, mem_norm_w, w_xq, w_xkv, w_xo, norm_mlp_w, w_mlp_up, w_mlp_down, norm_f_w):
    batch, seq, d = x_prompt.shape
    n_seq, t_dec, _ = x_sample.shape
    assert norm_mix_w.shape[0] == 1 and t_dec == SUBLANES and d == D_MODEL
    ab0 = 4 * D_MODEL
    w_in0 = w_in[0]
    w = {
        "main": jnp.concatenate([w_in0[:, :ab0], w_in0[:, ab0 + 2 * GDN_HEADS:]], axis=1).astype(BF16),
        "ab": w_in0[:, ab0:ab0 + 2 * GDN_HEADS],
        "gdn_o": w_gdn_o[0].astype(BF16), "sc_o": w_sc_o[0].astype(BF16), "mix": w_mix_out[0].astype(BF16),
        "xq": w_xq[0].astype(BF16), "xkv": w_xkv[0].astype(BF16), "xo": w_xo[0].astype(BF16),
        "up": w_mlp_up[0].astype(BF16), "down": w_mlp_down[0].astype(BF16),
        "norm_mlp": norm_mlp_w[0], "norm_f": norm_f_w,
    }
    tm, tn = 512, 1024

    rows_p = batch * seq
    xp = x_prompt.reshape(rows_p, d)
    memn = rmsnorm_rows(mem_prompt.reshape(batch * N_MEM, d), mem_norm_w[0], tm)
    mkv = matmul_rows(memn, w["xkv"], F32, tm, tn)
    xn, gb = rmsnorm_gates_rows(xp, norm_mix_w[0], w["ab"], gdn_A_log[0], gdn_dt_bias[0], tm)
    proj = matmul_rows(xn, w["main"], BF16, tm, tn)
    q, k, v, u_sc, cx_tail = prep_prompt(proj, gdn_conv_w[0], sc_conv_w[0], batch, seq, 256)
    o_gdn, s_prompt = gdn_prompt(q, k, v, proj, gb, gdn_out_norm_w[0], batch, seq, 4)
    x1, xn1 = merge_mixers(o_gdn, u_sc, proj, xp, w["gdn_o"], w["sc_o"], w["mix"], norm_x_w[0], tm)
    mkv3 = mkv.reshape(batch, N_MEM, 2 * d)
    y_prompt = _layer_tail(x1, xn1, lambda hq: attention_prompt(hq, mkv3, batch, seq, tm), w, tm, tn)
    p_conv = proj.reshape(batch, seq, MAIN_COLS)[:, seq - (GDN_CONV_W - 1):, :3 * d].astype(F32)
    p_sc = cx_tail[:, SUBLANES - (SC_CONV_W - 1):]
    p_mk = mkv3[:, :, :d].reshape(batch, N_MEM, X_HEADS, X_HEAD_DIM)
    p_mv = mkv3[:, :, d:].reshape(batch, N_MEM, X_HEADS, X_HEAD_DIM)

    rows_s = n_seq * t_dec
    xs = x_sample.reshape(rows_s, d)
    xn_s, gb_s = rmsnorm_gates_rows(xs, norm_mix_w[0], w["ab"], gdn_A_log[0], gdn_dt_bias[0], tm)
    proj_s = matmul_rows(xn_s, w["main"], F32, tm, tn)
    qst8 = jnp.pad(state_gdn_conv[0], ((0, 0), (0, SUBLANES - (GDN_CONV_W - 1)), (0, 0))).reshape(rows_s, 3 * d)
    cst8 = jnp.pad(state_sc_conv[0], ((0, 0), (0, SUBLANES - (SC_CONV_W - 1)), (0, 0))).reshape(rows_s, d)
    q_s, k_s, v_s, u_s, cx_s = prep_sample(proj_s, qst8, cst8, gdn_conv_w[0], sc_conv_w[0], n_seq, 16)
    o_s, s_sample = gdn_sample(q_s, k_s, v_s, proj_s, gb_s, gdn_out_norm_w[0], state_gdn[0], n_seq, 4)
    x1_s, xn1_s = merge_mixers(o_s, u_s, proj_s, xs, w["gdn_o"], w["sc_o"], w["mix"], norm_x_w[0], tm)
    y_sample = _layer_tail(
        x1_s, xn1_s, lambda hq: attention_sample(hq.astype(F32), cache_mem_k[0], cache_mem_v[0], n_seq, 4), w, tm, tn)
    s_conv = proj_s.reshape(n_seq, t_dec, MAIN_COLS)[:, t_dec - (GDN_CONV_W - 1):, :3 * d]
    s_sc = cx_s.reshape(n_seq, t_dec, d)[:, t_dec - (SC_CONV_W - 1):]

    return (y_prompt.reshape(batch, seq, d), y_sample.reshape(n_seq, t_dec, d),
            p_mk[None], p_mv[None], p_conv[None], s_prompt[None], p_sc[None],
            s_conv[None], s_sample[None], s_sc[None])
```

```python
import functools

import jax
import jax.numpy as jnp
from jax import lax
from jax.experimental import pallas as pl
from jax.experimental.pallas import tpu as pltpu

F32 = jnp.float32
BF16 = jnp.bfloat16

D_MODEL = 1024
GDN_HEADS = 8
HEAD_DIM = 128
GDN_CHUNK = 64
GDN_CONV_W = 4
SC_CONV_W = 3
X_HEADS = 4
X_HEAD_DIM = 256
N_MEM = 256
D_FF = 4096
EPS = 1e-6
SUBLANES = 8
LANES = 128
VMEM_LIMIT = 52 * 1024 * 1024


def _params(n_axes):
    return pltpu.CompilerParams(dimension_semantics=("arbitrary",) * n_axes, vmem_limit_bytes=VMEM_LIMIT)


def _dot(a, b):
    return jnp.dot(a.astype(BF16), b.astype(BF16), preferred_element_type=F32)


def _dot_nt(a, b):
    return lax.dot_general(a.astype(BF16), b.astype(BF16), (((1,), (1,)), ((), ())), preferred_element_type=F32)


def _dot_tn(a, b):
    return lax.dot_general(a.astype(BF16), b.astype(BF16), (((0,), (0,)), ((), ())), preferred_element_type=F32)


def _split_hi_lo(x):
    hi = x.astype(BF16)
    lo = (x - hi.astype(F32)).astype(BF16)
    return hi, lo


def _dot_exact_lhs(m01, x):
    hi, lo = _split_hi_lo(x)
    m = m01.astype(BF16)
    return jnp.dot(m, hi, preferred_element_type=F32) + jnp.dot(m, lo, preferred_element_type=F32)


def _rms(x, w):
    return x * lax.rsqrt(jnp.mean(x * x, axis=-1, keepdims=True) + EPS) * w


def _sigmoid(x):
    return 1.0 / (1.0 + jnp.exp(-x))


def _silu(x):
    return x * _sigmoid(x)


def _softplus(x):
    return jnp.maximum(x, 0.0) + jnp.log(1.0 + jnp.exp(-jnp.abs(x)))


def _shifted(x, prev_tail, s):
    rolled = pltpu.roll(x, s, 0)
    head = pltpu.roll(prev_tail, s, 0)
    row = lax.broadcasted_iota(jnp.int32, head.shape, 0)
    first = jnp.where(row < s, head, rolled[:SUBLANES])
    if x.shape[0] == SUBLANES:
        return first
    return jnp.concatenate([first, rolled[SUBLANES:]], axis=0)


def _shifted_groups(x, hist, s, n_hist):
    rows = x.shape[0]
    rolled = pltpu.roll(x, s, 0)
    head = hist if n_hist == s else pltpu.roll(hist, rows - (n_hist - s), 0)
    row = lax.broadcasted_iota(jnp.int32, x.shape, 0)
    return jnp.where((row & (SUBLANES - 1)) < s, head, rolled)


def _causal_conv(x, w, shift_fn):
    width = w.shape[0]
    y = x * w[width - 1:width]
    for i in range(width - 1):
        y = y + shift_fn(width - 1 - i) * w[i:i + 1]
    return y


def _l2norm_heads(x, scale):
    outs = []
    for h in range(x.shape[1] // HEAD_DIM):
        xh = x[:, h * HEAD_DIM:(h + 1) * HEAD_DIM]
        outs.append(xh * (lax.rsqrt(jnp.sum(xh * xh, axis=-1, keepdims=True) + EPS) * scale))
    return jnp.concatenate(outs, axis=1)


def _proj_qkv_kernel(*refs, grouped):
    if grouped:
        x_ref, nw_ref, w_ref, wab_ref, alog_ref, dtb_ref, cw_ref, hist_ref, q_ref, k_ref, v_ref, gb_ref, raw_ref = refs
    else:
        x_ref, nw_ref, w_ref, wab_ref, alog_ref, dtb_ref, cw_ref, q_ref, k_ref, v_ref, gb_ref, tail_ref, tail = refs

        @pl.when(pl.program_id(1) == 0)
        def _():
            tail[...] = jnp.zeros_like(tail)

    d = D_MODEL
    xn = _rms(x_ref[...], nw_ref[...]).astype(BF16)
    ab = jnp.dot(xn, wab_ref[...], preferred_element_type=F32)
    g = -jnp.exp(alog_ref[...]) * _softplus(ab[:, :GDN_HEADS] + dtb_ref[...])
    gb_ref[...] = jnp.concatenate([g, _sigmoid(ab[:, GDN_HEADS:2 * GDN_HEADS])], axis=1)
    tm = xn.shape[0]
    for c, out_ref in enumerate((q_ref, k_ref, v_ref)):
        cols = slice(c * d, (c + 1) * d)
        acc = jnp.dot(xn, w_ref[:, cols], preferred_element_type=F32)
        if grouped:
            raw_ref[:, cols] = acc
            shift = lambda s, acc=acc, cols=cols: _shifted_groups(acc, hist_ref[:, cols], s, GDN_CONV_W - 1)
        else:
            old = tail[:, cols]
            shift = lambda s, acc=acc, old=old: _shifted(acc, old, s)
        y = _silu(_causal_conv(acc, cw_ref[:, cols], shift))
        if c == 0:
            y = _l2norm_heads(y, HEAD_DIM ** -0.5)
        elif c == 1:
            y = _l2norm_heads(y, 1.0)
        out_ref[...] = y.astype(out_ref.dtype)
        if not grouped:
            tail[:, cols] = acc[tm - SUBLANES:]
            tail_ref[0, :, cols] = acc[tm - SUBLANES:]


def _gate_operands(w_ab, a_log, dt_bias):
    wab = jnp.zeros((D_MODEL, LANES), BF16).at[:, :2 * GDN_HEADS].set(w_ab.astype(BF16))
    return wab, a_log.reshape(1, GDN_HEADS), dt_bias.reshape(1, GDN_HEADS)


def proj_qkv_prompt(x, norm_w, w_qkv, w_ab, a_log, dt_bias, conv_w, batch, seq, tm):
    d = D_MODEL
    nt = seq // tm
    rows = batch * seq
    row = lambda b, t: (b * nt + t, 0)
    full = lambda b, t: (0, 0)
    return pl.pallas_call(
        functools.partial(_proj_qkv_kernel, grouped=False),
        grid=(batch, nt),
        in_specs=[
            pl.BlockSpec((tm, d), row), pl.BlockSpec((1, d), full), pl.BlockSpec((d, 3 * d), full),
            pl.BlockSpec((d, LANES), full), pl.BlockSpec((1, GDN_HEADS), full), pl.BlockSpec((1, GDN_HEADS), full),
            pl.BlockSpec((GDN_CONV_W, 3 * d), full),
        ],
        out_specs=[pl.BlockSpec((tm, d), row)] * 3 + [
            pl.BlockSpec((tm, 2 * GDN_HEADS), row), pl.BlockSpec((1, SUBLANES, 3 * d), lambda b, t: (b, 0, 0))],
        out_shape=[jax.ShapeDtypeStruct((rows, d), BF16)] * 3 + [
            jax.ShapeDtypeStruct((rows, 2 * GDN_HEADS), F32), jax.ShapeDtypeStruct((batch, SUBLANES, 3 * d), F32)],
        scratch_shapes=[pltpu.VMEM((SUBLANES, 3 * d), F32)],
        compiler_params=_params(2),
        name="proj_qkv_prompt",
    )(x, norm_w.reshape(1, d), w_qkv, *_gate_operands(w_ab, a_log, dt_bias), conv_w)


def proj_qkv_sample(x, norm_w, w_qkv, w_ab, a_log, dt_bias, conv_w, hist, tm):
    d = D_MODEL
    rows = x.shape[0]
    row = lambda i: (i, 0)
    full = lambda i: (0, 0)
    return pl.pallas_call(
        functools.partial(_proj_qkv_kernel, grouped=True),
        grid=(rows // tm,),
        in_specs=[
            pl.BlockSpec((tm, d), row), pl.BlockSpec((1, d), full), pl.BlockSpec((d, 3 * d), full),
            pl.BlockSpec((d, LANES), full), pl.BlockSpec((1, GDN_HEADS), full), pl.BlockSpec((1, GDN_HEADS), full),
            pl.BlockSpec((GDN_CONV_W, 3 * d), full), pl.BlockSpec((tm, 3 * d), row),
        ],
        out_specs=[pl.BlockSpec((tm, d), row)] * 3 + [
            pl.BlockSpec((tm, 2 * GDN_HEADS), row), pl.BlockSpec((tm, 3 * d), row)],
        out_shape=[jax.ShapeDtypeStruct((rows, d), F32)] * 3 + [
            jax.ShapeDtypeStruct((rows, 2 * GDN_HEADS), F32), jax.ShapeDtypeStruct((rows, 3 * d), F32)],
        compiler_params=_params(1),
        name="proj_qkv_sample",
    )(x, norm_w.reshape(1, d), w_qkv, *_gate_operands(w_ab, a_log, dt_bias), conv_w, hist)


def _proj_sc_kernel(*refs, grouped):
    if grouped:
        x_ref, nw_ref, w_ref, sw_ref, hist_ref, u_ref, cx_ref = refs
    else:
        x_ref, nw_ref, w_ref, sw_ref, u_ref, tail_ref, tail = refs

        @pl.when(pl.program_id(1) == 0)
        def _():
            tail[...] = jnp.zeros_like(tail)

    d = D_MODEL
    half = d // 2
    xn = _rms(x_ref[...], nw_ref[...]).astype(BF16)
    tm = xn.shape[0]
    for c in range(2):
        cols = slice(c * half, (c + 1) * half)
        part = lambda j: jnp.dot(xn, w_ref[:, j * d + c * half:j * d + (c + 1) * half], preferred_element_type=F32)
        cx = part(1) * part(2)
        if grouped:
            cx_ref[:, cols] = cx
            shift = lambda s, cx=cx, cols=cols: _shifted_groups(cx, hist_ref[:, cols], s, SC_CONV_W - 1)
        else:
            old = tail[:, cols]
            shift = lambda s, cx=cx, old=old: _shifted(cx, old, s)
        u_ref[:, cols] = (part(0) * _causal_conv(cx, sw_ref[:, cols], shift)).astype(u_ref.dtype)
        if not grouped:
            tail[:, cols] = cx[tm - SUBLANES:]
            tail_ref[0, :, cols] = cx[tm - SUBLANES:]


def proj_sc_prompt(x, norm_w, w_sc, sc_w, batch, seq, tm):
    d = D_MODEL
    nt = seq // tm
    rows = batch * seq
    row = lambda b, t: (b * nt + t, 0)
    full = lambda b, t: (0, 0)
    return pl.pallas_call(
        functools.partial(_proj_sc_kernel, grouped=False),
        grid=(batch, nt),
        in_specs=[pl.BlockSpec((tm, d), row), pl.BlockSpec((1, d), full), pl.BlockSpec((d, 3 * d), full),
                  pl.BlockSpec((SC_CONV_W, d), full)],
        out_specs=[pl.BlockSpec((tm, d), row), pl.BlockSpec((1, SUBLANES, d), lambda b, t: (b, 0, 0))],
        out_shape=[jax.ShapeDtypeStruct((rows, d), BF16), jax.ShapeDtypeStruct((batch, SUBLANES, d), F32)],
        scratch_shapes=[pltpu.VMEM((SUBLANES, d), F32)],
        compiler_params=_params(2),
        name="proj_sc_prompt",
    )(x, norm_w.reshape(1, d), w_sc, sc_w)


def proj_sc_sample(x, norm_w, w_sc, sc_w, hist, tm):
    d = D_MODEL
    rows = x.shape[0]
    row = lambda i: (i, 0)
    full = lambda i: (0, 0)
    return pl.pallas_call(
        functools.partial(_proj_sc_kernel, grouped=True),
        grid=(rows // tm,),
        in_specs=[pl.BlockSpec((tm, d), row), pl.BlockSpec((1, d), full), pl.BlockSpec((d, 3 * d), full),
                  pl.BlockSpec((SC_CONV_W, d), full), pl.BlockSpec((tm, d), row)],
        out_specs=[pl.BlockSpec((tm, d), row)] * 2,
        out_shape=[jax.ShapeDtypeStruct((rows, d), F32)] * 2,
        compiler_params=_params(1),
        name="proj_sc_sample",
    )(x, norm_w.reshape(1, d), w_sc, sc_w, hist)


def _proj_gates_kernel(x_ref, nw_ref, w_ref, zs_ref, sa_ref, sb_ref):
    d = D_MODEL
    xn = _rms(x_ref[...], nw_ref[...]).astype(BF16)
    zs_ref[...] = _silu(jnp.dot(xn, w_ref[:, :d], preferred_element_type=F32)).astype(zs_ref.dtype)
    sa_ref[...] = _sigmoid(jnp.dot(xn, w_ref[:, d:2 * d], preferred_element_type=F32)).astype(sa_ref.dtype)
    sb_ref[...] = _sigmoid(jnp.dot(xn, w_ref[:, 2 * d:], preferred_element_type=F32)).astype(sb_ref.dtype)


def proj_gates(x, norm_w, w_zab, out_dtype, tm):
    rows, d = x.shape
    row = lambda i: (i, 0)
    full = lambda i: (0, 0)
    return pl.pallas_call(
        _proj_gates_kernel,
        grid=(rows // tm,),
        in_specs=[pl.BlockSpec((tm, d), row), pl.BlockSpec((1, d), full), pl.BlockSpec((d, 3 * d), full)],
        out_specs=[pl.BlockSpec((tm, d), row)] * 3,
        out_shape=[jax.ShapeDtypeStruct((rows, d), out_dtype)] * 3,
        compiler_params=_params(1),
        name="proj_gates",
    )(x, norm_w.reshape(1, d), w_zab)


def _norm_mm_kernel(x_ref, nw_ref, w_ref, *out_refs):
    acc = _dot(_rms(x_ref[...], nw_ref[...]), w_ref[...])
    width = acc.shape[1] // len(out_refs)
    for i, o_ref in enumerate(out_refs):
        o_ref[...] = acc[:, i * width:(i + 1) * width].astype(o_ref.dtype)


def norm_matmul_split(x, norm_w, w, n_out, out_dtype, tm):
    rows, d = x.shape
    n = w.shape[1]
    return pl.pallas_call(
        _norm_mm_kernel,
        grid=(rows // tm,),
        in_specs=[pl.BlockSpec((tm, d), lambda i: (i, 0)), pl.BlockSpec((1, d), lambda i: (0, 0)),
                  pl.BlockSpec((d, n), lambda i: (0, 0))],
        out_specs=[pl.BlockSpec((tm, n // n_out), lambda i: (i, 0))] * n_out,
        out_shape=[jax.ShapeDtypeStruct((rows, n // n_out), out_dtype)] * n_out,
        compiler_params=_params(1),
        name="norm_matmul_split",
    )(x, norm_w.reshape(1, d), w)


def _mm_kernel(a_ref, w_ref, o_ref, *, relu2):
    acc = _dot(a_ref[...], w_ref[...])
    if relu2:
        acc = jnp.square(jnp.maximum(acc, 0.0))
    o_ref[...] = acc.astype(o_ref.dtype)


def matmul_rows(a, w, out_dtype, tm, tn, relu2=False):
    rows, k = a.shape
    n = w.shape[1]
    return pl.pallas_call(
        functools.partial(_mm_kernel, relu2=relu2),
        grid=(n // tn, rows // tm),
        in_specs=[pl.BlockSpec((tm, k), lambda j, i: (i, 0)), pl.BlockSpec((k, tn), lambda j, i: (0, j))],
        out_specs=pl.BlockSpec((tm, tn), lambda j, i: (i, j)),
        out_shape=jax.ShapeDtypeStruct((rows, n), out_dtype),
        compiler_params=_params(2),
        name="matmul_rows",
    )(a, w)


def _mm_res_norm_kernel(a_ref, w_ref, res_ref, nw_ref, x_ref, xn_ref):
    x = res_ref[...] + _dot(a_ref[...], w_ref[...])
    x_ref[...] = x
    xn_ref[...] = _rms(x, nw_ref[...]).astype(xn_ref.dtype)


def matmul_residual_norm(a, w, res, norm_w, xn_dtype, tm):
    rows, k = a.shape
    n = w.shape[1]
    return pl.pallas_call(
        _mm_res_norm_kernel,
        grid=(rows // tm,),
        in_specs=[
            pl.BlockSpec((tm, k), lambda i: (i, 0)),
            pl.BlockSpec((k, n), lambda i: (0, 0)),
            pl.BlockSpec((tm, n), lambda i: (i, 0)),
            pl.BlockSpec((1, n), lambda i: (0, 0)),
        ],
        out_specs=[pl.BlockSpec((tm, n), lambda i: (i, 0)), pl.BlockSpec((tm, n), lambda i: (i, 0))],
        out_shape=[jax.ShapeDtypeStruct((rows, n), F32), jax.ShapeDtypeStruct((rows, n), xn_dtype)],
        compiler_params=_params(1),
        name="matmul_residual_norm",
    )(a, w, res, norm_w.reshape(1, n))


def _merge_kernel(oa_ref, ub_ref, sa_ref, sb_ref, x_ref, wa_ref, wb_ref, wm_ref, nw_ref, x1_ref, xn_ref):
    ya = _dot(oa_ref[...], wa_ref[...])
    yb = _dot(ub_ref[...], wb_ref[...])
    merged = sa_ref[...].astype(F32) * ya + sb_ref[...].astype(F32) * yb
    x1 = x_ref[...] + _dot(merged, wm_ref[...])
    x1_ref[...] = x1
    xn_ref[...] = _rms(x1, nw_ref[...]).astype(xn_ref.dtype)


def merge_mixers(o_a, u_b, sig_a, sig_b, x, w_gdn_o, w_sc_o, w_mix_out, norm_w, tm):
    rows, d = x.shape
    row_blk = lambda i: (i, 0)
    full = lambda i: (0, 0)
    return pl.pallas_call(
        _merge_kernel,
        grid=(rows // tm,),
        in_specs=[pl.BlockSpec((tm, d), row_blk)] * 5 + [pl.BlockSpec((d, d), full)] * 3 + [pl.BlockSpec((1, d), full)],
        out_specs=[pl.BlockSpec((tm, d), row_blk), pl.BlockSpec((tm, d), row_blk)],
        out_shape=[jax.ShapeDtypeStruct((rows, d), F32), jax.ShapeDtypeStruct((rows, d), BF16)],
        compiler_params=_params(1),
        name="merge_mixers",
    )(o_a, u_b, sig_a, sig_b, x, w_gdn_o, w_sc_o, w_mix_out, norm_w.reshape(1, d))


def _mm_res_final_kernel(a_ref, w_ref, res_ref, nw_ref, y_ref):
    x = res_ref[...] + _dot(a_ref[...], w_ref[...])
    y_ref[...] = _rms(x, nw_ref[...])


def matmul_residual_final_norm(a, w, res, norm_w, tm):
    rows, k = a.shape
    n = w.shape[1]
    return pl.pallas_call(
        _mm_res_final_kernel,
        grid=(rows // tm,),
        in_specs=[
            pl.BlockSpec((tm, k), lambda i: (i, 0)),
            pl.BlockSpec((k, n), lambda i: (0, 0)),
            pl.BlockSpec((tm, n), lambda i: (i, 0)),
            pl.BlockSpec((1, n), lambda i: (0, 0)),
        ],
        out_specs=pl.BlockSpec((tm, n), lambda i: (i, 0)),
        out_shape=jax.ShapeDtypeStruct((rows, n), F32),
        compiler_params=_params(1),
        name="matmul_residual_final_norm",
    )(a, w, res, norm_w.reshape(1, n))


def _tri_inverse(l_mats, n, top, tick):
    ri = lax.broadcasted_iota(jnp.int32, (n, n), 0)
    ci = lax.broadcasted_iota(jnp.int32, (n, n), 1)
    eye = jnp.where(ri == ci, 1.0, 0.0)
    ts = [eye - jnp.where((ri ^ ci) == 1, l, 0.0) for l in l_mats]
    shift = 1
    while (1 << shift) < top:
        pair = ((ri >> shift) ^ (ci >> shift)) == 1
        xs = [_dot(jnp.where(pair, l, 0.0), t) for l, t in zip(l_mats, ts)]
        tick()
        ts = [t - _dot(t, x) for t, x in zip(ts, xs)]
        tick()
        shift += 1
    return ts


def _pair_diff(col_b):
    hi, lo = _split_hi_lo(col_b)
    hi = hi.astype(F32)
    lo = lo.astype(F32)
    lane = lax.broadcasted_iota(jnp.int32, col_b.shape, 1)
    left = jnp.where(lane == 0, hi, jnp.where(lane == 1, lo, jnp.where(lane < 4, 1.0, 0.0)))
    right = jnp.where(lane == 2, -hi, jnp.where(lane == 3, -lo, jnp.where(lane < 2, 1.0, 0.0)))
    return _dot_nt(left, right)


def _gdn_intra(qs, ks, vs, gcum_bs, beta_bs, block, tick=lambda: None):
    n = qs[0].shape[0]
    ri = lax.broadcasted_iota(jnp.int32, (n, n), 0)
    ci = lax.broadcasted_iota(jnp.int32, (n, n), 1)
    same = (ri >> (block.bit_length() - 1)) == (ci >> (block.bit_length() - 1))
    diffs = [_pair_diff(g) for g in gcum_bs]
    tick()
    decays = [jnp.where(same, jnp.where(ci <= ri, jnp.exp(jnp.minimum(d, 0.0)), 0.0), 0.0) for d in diffs]
    kks = [_dot_nt(k, k) for k in ks]
    tick()
    l_mats = [b[:, :n] * kk * jnp.where(ci < ri, dec, 0.0) for b, kk, dec in zip(beta_bs, kks, decays)]
    ts = _tri_inverse(l_mats, n, block, tick)
    egcs = [jnp.exp(g) for g in gcum_bs]
    wus = [_dot(t, jnp.concatenate([k * (b * e), v * b], axis=1))
           for t, k, v, b, e in zip(ts, ks, vs, beta_bs, egcs)]
    tick()
    ps = [_dot_nt(q, k) * dec for q, k, dec in zip(qs, ks, decays)]
    tick()
    return ([wu[:, :HEAD_DIM] for wu in wus], [wu[:, HEAD_DIM:] for wu in wus],
            [q * e for q, e in zip(qs, egcs)], ps)


def _gated_out(o, zs, onw):
    return _rms(o, onw) * zs


def _gdn_prompt_kernel(q_ref, k_ref, v_ref, gb_ref, zs_ref, onw_ref, o_ref, s_out_ref,
                       s_ref, wqe_scr, p_scr, ke_scr, uv_scr, eg_scr, *, n_chunks, blocks_per_seq):
    g = pl.program_id(0)
    n = GDN_CHUNK
    heads = range(GDN_HEADS)
    chunks = range(n_chunks)
    sls = [slice(h * HEAD_DIM, (h + 1) * HEAD_DIM) for h in heads]

    @pl.when(g == 0)
    def _():
        s_ref[...] = jnp.zeros_like(s_ref)
        wqe_scr[...] = jnp.zeros_like(wqe_scr)
        p_scr[...] = jnp.zeros_like(p_scr)
        ke_scr[...] = jnp.zeros_like(ke_scr)
        uv_scr[...] = jnp.zeros_like(uv_scr)
        eg_scr[...] = jnp.zeros_like(eg_scr)

    wslot = lax.rem(g, 2)
    rslot = 1 - wslot
    first_of_seq = lax.rem(g + blocks_per_seq - 1, blocks_per_seq) == 0
    state = {"s": [jnp.where(first_of_seq, 0.0, s_ref[h]) for h in heads]}

    def chain_a(c):
        state["wq"] = [_dot(wqe_scr[rslot, c, h], state["s"][h]) for h in heads]

    def chain_b(c):
        rows = slice(c * n, (c + 1) * n)
        us = [uv_scr[rslot, c, h] - state["wq"][h][:n] for h in heads]
        pus = [_dot(p_scr[rslot, c, h], us[h]) for h in heads]
        kus = [_dot_tn(ke_scr[rslot, c, h], us[h]) for h in heads]
        state["s"] = [eg_scr[rslot, c, h][0:1, :] * state["s"][h] + kus[h] for h in heads]
        for h in heads:
            o = state["wq"][h][n:] + pus[h]
            o_ref[rows, sls[h]] = _gated_out(o, zs_ref[rows, sls[h]].astype(F32), onw_ref[...]).astype(o_ref.dtype)

    pending = [functools.partial(f, c) for c in chunks for f in (chain_a, chain_b)]

    def tick():
        if pending:
            pending.pop(0)()

    ri = lax.broadcasted_iota(jnp.int32, (n, n), 0)
    ci = lax.broadcasted_iota(jnp.int32, (n, n), 1)
    tri = jnp.where(ci <= ri, 1.0, 0.0)
    probs = [(c, h) for c in chunks for h in heads]
    gbs = [gb_ref[c * n:(c + 1) * n, :] for c in chunks]
    gcums = [_dot_exact_lhs(tri, gb) for gb in gbs]
    tick()
    qs = [q_ref[c * n:(c + 1) * n, sls[h]].astype(F32) for c, h in probs]
    ks = [k_ref[c * n:(c + 1) * n, sls[h]].astype(F32) for c, h in probs]
    vs = [v_ref[c * n:(c + 1) * n, sls[h]].astype(F32) for c, h in probs]
    gcum_bs = [jnp.broadcast_to(gcums[c][:, h:h + 1], (n, HEAD_DIM)) for c, h in probs]
    beta_bs = [jnp.broadcast_to(gbs[c][:, GDN_HEADS + h:GDN_HEADS + h + 1], (n, HEAD_DIM)) for c, h in probs]
    ws, uvs, qes, ps = _gdn_intra(qs, ks, vs, gcum_bs, beta_bs, n, tick)
    for i, (c, h) in enumerate(probs):
        glast = gcum_bs[i][n - 1:n, :]
        wqe_scr[wslot, c, h] = jnp.concatenate([ws[i], qes[i]], axis=0).astype(BF16)
        uv_scr[wslot, c, h] = uvs[i]
        p_scr[wslot, c, h] = ps[i].astype(BF16)
        ke_scr[wslot, c, h] = (ks[i] * jnp.exp(glast - gcum_bs[i])).astype(BF16)
        eg_scr[wslot, c, h] = jnp.broadcast_to(jnp.exp(glast), (SUBLANES, HEAD_DIM))
    while pending:
        tick()
    for h in heads:
        s_ref[h] = state["s"][h]
        s_out_ref[0, h] = state["s"][h]


def gdn_prompt(q, k, v, zs, gb, out_norm_w, batch, seq, n_chunks):
    rows = batch * seq
    n = GDN_CHUNK
    tm = n * n_chunks
    blocks_per_seq = seq // tm
    n_blocks = rows // tm
    d = D_MODEL
    cur = lambda g: (jnp.minimum(g, n_blocks - 1), 0)
    prev = lambda g: (jnp.maximum(g - 1, 0), 0)
    per_head = lambda lead, dtype: pltpu.VMEM((2, n_chunks, GDN_HEADS) + lead, dtype)
    return pl.pallas_call(
        functools.partial(_gdn_prompt_kernel, n_chunks=n_chunks, blocks_per_seq=blocks_per_seq),
        grid=(n_blocks + 1,),
        in_specs=[
            pl.BlockSpec((tm, d), cur), pl.BlockSpec((tm, d), cur), pl.BlockSpec((tm, d), cur),
            pl.BlockSpec((tm, 2 * GDN_HEADS), cur),
            pl.BlockSpec((tm, d), prev),
            pl.BlockSpec((1, HEAD_DIM), lambda g: (0, 0)),
        ],
        out_specs=[pl.BlockSpec((tm, d), prev),
                   pl.BlockSpec((1, GDN_HEADS, HEAD_DIM, HEAD_DIM),
                                lambda g: (jnp.maximum(g - 1, 0) // blocks_per_seq, 0, 0, 0))],
        out_shape=[jax.ShapeDtypeStruct((rows, d), BF16),
                   jax.ShapeDtypeStruct((batch, GDN_HEADS, HEAD_DIM, HEAD_DIM), F32)],
        scratch_shapes=[
            pltpu.VMEM((GDN_HEADS, HEAD_DIM, HEAD_DIM), F32),
            per_head((2 * n, HEAD_DIM), BF16),
            per_head((n, n), BF16),
            per_head((n, HEAD_DIM), BF16),
            per_head((n, HEAD_DIM), F32),
            per_head((SUBLANES, HEAD_DIM), F32),
        ],
        compiler_params=_params(1),
        name="gdn_prompt",
    )(q, k, v, gb, zs, out_norm_w.reshape(1, HEAD_DIM))


def _stack_heads(x):
    return jnp.concatenate([x[:, h * HEAD_DIM:(h + 1) * HEAD_DIM] for h in range(GDN_HEADS)], axis=0)


def _gdn_sample_kernel(q_ref, k_ref, v_ref, zs_ref, gb_ref, onw_ref, s_in_ref, o_ref, s_out_ref, *, n_seq):
    t = SUBLANES
    n = GDN_HEADS * t
    seqs = range(n_seq)
    heads = range(GDN_HEADS)
    rows = [slice(i * t, (i + 1) * t) for i in seqs]
    qs = [_stack_heads(q_ref[r, :]) for r in rows]
    ks = [_stack_heads(k_ref[r, :]) for r in rows]
    vs = [_stack_heads(v_ref[r, :]) for r in rows]
    gbs = [gb_ref[r, :] for r in rows]
    g_bs = [jnp.concatenate([jnp.broadcast_to(gb[:, h:h + 1], (t, HEAD_DIM)) for h in heads], axis=0) for gb in gbs]
    beta_bs = [jnp.concatenate([jnp.broadcast_to(gb[:, GDN_HEADS + h:GDN_HEADS + h + 1], (t, HEAD_DIM))
                                for h in heads], axis=0) for gb in gbs]
    ri = lax.broadcasted_iota(jnp.int32, (n, n), 0)
    ci = lax.broadcasted_iota(jnp.int32, (n, n), 1)
    cum_mask = jnp.where(((ri >> 3) == (ci >> 3)), jnp.where(ci <= ri, 1.0, 0.0), 0.0)
    gcum_bs = [_dot_exact_lhs(cum_mask, g) for g in g_bs]
    ws, uvs, qes, ps = _gdn_intra(qs, ks, vs, gcum_bs, beta_bs, t)
    glast_bs = [jnp.concatenate([jnp.broadcast_to(g[h * t + t - 1:h * t + t, :], (t, HEAD_DIM)) for h in heads],
                                axis=0) for g in gcum_bs]
    kes = [k * jnp.exp(gl - g) for k, gl, g in zip(ks, glast_bs, gcum_bs)]
    wqes = [jnp.concatenate([w, qe], axis=0) for w, qe in zip(ws, qes)]
    wqs = [[_dot(wqes[i], s_in_ref[i, h]) for h in heads] for i in seqs]
    us = [uvs[i] - jnp.concatenate([wqs[i][h][h * t:(h + 1) * t] for h in heads], axis=0) for i in seqs]
    os_ = [jnp.concatenate([wqs[i][h][n + h * t:n + (h + 1) * t] for h in heads], axis=0) + _dot(ps[i], us[i])
           for i in seqs]
    row = lax.broadcasted_iota(jnp.int32, (n, HEAD_DIM), 0)
    for i in seqs:
        for h in heads:
            u_h = jnp.where((row >> 3) == h, us[i], 0.0)
            s_out_ref[i, h] = jnp.exp(glast_bs[i][h * t:h * t + 1, :]) * s_in_ref[i, h] + _dot_tn(kes[i], u_h)
    for i in seqs:
        og = _gated_out(os_[i], _stack_heads(zs_ref[rows[i], :]), onw_ref[...])
        for h in heads:
            o_ref[rows[i], h * HEAD_DIM:(h + 1) * HEAD_DIM] = og[h * t:(h + 1) * t]


def gdn_sample(q, k, v, zs, gb, out_norm_w, state, n_seq, per_step):
    t = SUBLANES
    d = D_MODEL
    rowblk = pl.BlockSpec((per_step * t, d), lambda i: (i, 0))
    sblk = pl.BlockSpec((per_step, GDN_HEADS, HEAD_DIM, HEAD_DIM), lambda i: (i, 0, 0, 0))
    return pl.pallas_call(
        functools.partial(_gdn_sample_kernel, n_seq=per_step),
        grid=(n_seq // per_step,),
        in_specs=[
            rowblk, rowblk, rowblk, rowblk,
            pl.BlockSpec((per_step * t, 2 * GDN_HEADS), lambda i: (i, 0)),
            pl.BlockSpec((1, HEAD_DIM), lambda i: (0, 0)),
            sblk,
        ],
        out_specs=[rowblk, sblk],
        out_shape=[jax.ShapeDtypeStruct((n_seq * t, d), F32),
                   jax.ShapeDtypeStruct((n_seq, GDN_HEADS, HEAD_DIM, HEAD_DIM), F32)],
        compiler_params=_params(1),
        name="gdn_sample",
    )(q, k, v, zs, gb, out_norm_w.reshape(1, HEAD_DIM), state)


def _attn_prompt_kernel(q_ref, mk_ref, mv_ref, o_ref):
    for h in range(X_HEADS):
        cols = slice(h * X_HEAD_DIM, (h + 1) * X_HEAD_DIM)
        s = _dot_nt(q_ref[:, cols], mk_ref[0, :, cols]) * (X_HEAD_DIM ** -0.5)
        e = jnp.exp(s - jnp.max(s, axis=-1, keepdims=True))
        p = e / jnp.sum(e, axis=-1, keepdims=True)
        o_ref[:, cols] = _dot(p, mv_ref[0, :, cols]).astype(o_ref.dtype)


def attention_prompt(q, mk, mv, batch, seq, tm):
    nt = seq // tm
    d = D_MODEL
    mem = pl.BlockSpec((1, N_MEM, d), lambda b, t: (b, 0, 0))
    return pl.pallas_call(
        _attn_prompt_kernel,
        grid=(batch, nt),
        in_specs=[pl.BlockSpec((tm, d), lambda b, t: (b * nt + t, 0)), mem, mem],
        out_specs=pl.BlockSpec((tm, d), lambda b, t: (b * nt + t, 0)),
        out_shape=jax.ShapeDtypeStruct((batch * seq, d), BF16),
        compiler_params=_params(2),
        name="attention_prompt",
    )(q, mk, mv)


def _attn_sample_kernel(q_ref, mk_ref, mv_ref, o_ref, *, n_seq):
    t = SUBLANES
    n_rows = N_MEM * X_HEADS
    seqs = range(n_seq)
    heads = range(X_HEADS)
    scores = []
    for i in seqs:
        q = q_ref[i * t:(i + 1) * t, :]
        q_all = jnp.concatenate([q[:, h * X_HEAD_DIM:(h + 1) * X_HEAD_DIM] for h in heads], axis=0)
        scores.append(_dot_nt(q_all, mk_ref[i].reshape(n_rows, X_HEAD_DIM)) * (X_HEAD_DIM ** -0.5))
    row_head = lax.broadcasted_iota(jnp.int32, (X_HEADS * t, n_rows), 0) >> 3
    col_head = lax.broadcasted_iota(jnp.int32, (X_HEADS * t, n_rows), 1) & (X_HEADS - 1)
    same_head = row_head == col_head
    probs = []
    for s in scores:
        s = jnp.where(same_head, s, -jnp.inf)
        e = jnp.exp(s - jnp.max(s, axis=-1, keepdims=True))
        probs.append(e / jnp.sum(e, axis=-1, keepdims=True))
    ctxs = [_dot(p, mv_ref[i].reshape(n_rows, X_HEAD_DIM)) for p, i in zip(probs, seqs)]
    for c, i in zip(ctxs, seqs):
        for h in heads:
            o_ref[i * t:(i + 1) * t, h * X_HEAD_DIM:(h + 1) * X_HEAD_DIM] = c[h * t:(h + 1) * t]


def attention_sample(q, mem_k, mem_v, n_seq, per_step):
    t = SUBLANES
    d = D_MODEL
    mem = pl.BlockSpec((per_step, N_MEM, X_HEADS, X_HEAD_DIM), lambda i: (i, 0, 0, 0))
    return pl.pallas_call(
        functools.partial(_attn_sample_kernel, n_seq=per_step),
        grid=(n_seq // per_step,),
        in_specs=[pl.BlockSpec((per_step * t, d), lambda i: (i, 0)), mem, mem],
        out_specs=pl.BlockSpec((per_step * t, d), lambda i: (i, 0)),
        out_shape=jax.ShapeDtypeStruct((n_seq * t, d), F32),
        compiler_params=_params(1),
        name="attention_sample",
    )(q, mem_k, mem_v)


def _layer_tail(x1, xn1, ctx_fn, w, tm, tn):
    hq = matmul_rows(xn1, w["xq"], xn1.dtype, tm, tn)
    ctx = ctx_fn(hq)
    x2, xn2 = matmul_residual_norm(ctx, w["xo"], x1, w["norm_mlp"], BF16, tm)
    hm = matmul_rows(xn2, w["up"], BF16, tm, tn, relu2=True)
    return matmul_residual_final_norm(hm, w["down"], x2, w["norm_f"], min(tm, 512))


def kernel(x_prompt, x_sample, mem_prompt, cache_mem_k, cache_mem_v, state_gdn_conv, state_gdn, state_sc_conv, norm_mix_w, w_in, gdn_conv_w, gdn_A_log, gdn_dt_bias, gdn_out_norm_w, w_gdn_o, sc_conv_w, w_sc_o, w_mix_out, norm_x_w, mem_norm_w, w_xq, w_xkv, w_xo, norm_mlp_w, w_mlp_up, w_mlp_down, norm_f_w):
    batch, seq, d = x_prompt.shape
    n_seq, t_dec, _ = x_sample.shape
    assert norm_mix_w.shape[0] == 1 and t_dec == SUBLANES and d == D_MODEL
    w_in0 = w_in[0]
    ab0 = 4 * d
    sc0 = ab0 + 2 * GDN_HEADS
    cast = lambda a: a.astype(BF16)
    w = {
        "qkv": cast(w_in0[:, :3 * d]),
        "zab": cast(jnp.concatenate([w_in0[:, 3 * d:4 * d], w_in0[:, sc0 + 3 * d:]], axis=1)),
        "sc": cast(w_in0[:, sc0:sc0 + 3 * d]),
        "ab": w_in0[:, ab0:sc0],
        "gdn_o": cast(w_gdn_o[0]), "sc_o": cast(w_sc_o[0]), "mix": cast(w_mix_out[0]),
        "xq": cast(w_xq[0]), "xkv": cast(w_xkv[0]), "xo": cast(w_xo[0]),
        "up": cast(w_mlp_up[0]), "down": cast(w_mlp_down[0]),
        "norm_mlp": norm_mlp_w[0], "norm_f": norm_f_w,
    }
    gate_args = (w["ab"], gdn_A_log[0], gdn_dt_bias[0])
    tm, tn = 512, 1024

    rows_p = batch * seq
    xp = x_prompt.reshape(rows_p, d)
    p_mk, p_mv = norm_matmul_split(mem_prompt.reshape(batch * N_MEM, d), mem_norm_w[0], w["xkv"], 2, F32, tm)
    q, k, v, gb, qkv_tail = proj_qkv_prompt(xp, norm_mix_w[0], w["qkv"], *gate_args, gdn_conv_w[0], batch, seq, tm)
    zs, sig_a, sig_b = proj_gates(xp, norm_mix_w[0], w["zab"], BF16, tm)
    u_sc, cx_tail = proj_sc_prompt(xp, norm_mix_w[0], w["sc"], sc_conv_w[0], batch, seq, tm)
    o_gdn, s_prompt = gdn_prompt(q, k, v, zs, gb, gdn_out_norm_w[0], batch, seq, 4)
    x1, xn1 = merge_mixers(o_gdn, u_sc, sig_a, sig_b, xp, w["gdn_o"], w["sc_o"], w["mix"], norm_x_w[0], tm)
    mk3 = p_mk.reshape(batch, N_MEM, d)
    mv3 = p_mv.reshape(batch, N_MEM, d)
    y_prompt = _layer_tail(x1, xn1, lambda hq: attention_prompt(hq, mk3, mv3, batch, seq, tm), w, 1024, tn)
    p_conv = qkv_tail[:, SUBLANES - (GDN_CONV_W - 1):]
    p_sc = cx_tail[:, SUBLANES - (SC_CONV_W - 1):]

    rows_s = n_seq * t_dec
    xs = x_sample.reshape(rows_s, d)
    qst8 = jnp.pad(state_gdn_conv[0], ((0, 0), (0, SUBLANES - (GDN_CONV_W - 1)), (0, 0))).reshape(rows_s, 3 * d)
    cst8 = jnp.pad(state_sc_conv[0], ((0, 0), (0, SUBLANES - (SC_CONV_W - 1)), (0, 0))).reshape(rows_s, d)
    q_s, k_s, v_s, gb_s, raw_s = proj_qkv_sample(xs, norm_mix_w[0], w["qkv"], *gate_args, gdn_conv_w[0], qst8, tm)
    zs_s, sig_a_s, sig_b_s = proj_gates(xs, norm_mix_w[0], w["zab"], F32, tm)
    u_s, cx_s = proj_sc_sample(xs, norm_mix_w[0], w["sc"], sc_conv_w[0], cst8, tm)
    o_s, s_sample = gdn_sample(q_s, k_s, v_s, zs_s, gb_s, gdn_out_norm_w[0], state_gdn[0], n_seq, 4)
    x1_s, xn1_s = merge_mixers(o_s, u_s, sig_a_s, sig_b_s, xs, w["gdn_o"], w["sc_o"], w["mix"], norm_x_w[0], tm)
    y_sample = _layer_tail(
        x1_s, xn1_s, lambda hq: attention_sample(hq.astype(F32), cache_mem_k[0], cache_mem_v[0], n_seq, 4), w, tm, tn)
    s_conv = raw_s.reshape(n_seq, t_dec, 3 * d)[:, t_dec - (GDN_CONV_W - 1):]
    s_sc = cx_s.reshape(n_seq, t_dec, d)[:, t_dec - (SC_CONV_W - 1):]

    return (y_prompt.reshape(batch, seq, d), y_sample.reshape(n_seq, t_dec, d),
            p_mk.reshape(1, batch, N_MEM, X_HEADS, X_HEAD_DIM), p_mv.reshape(1, batch, N_MEM, X_HEADS, X_HEAD_DIM),
            p_conv[None], s_prompt[None], p_sc[None], s_conv[None], s_sample[None], s_sc[None])
```
